```python
import math
import jax
import jax.numpy as jnp
from jax import lax
import numpy as np

D_MODEL = 1024
BATCH = 16
SEQ = 2048
DEPTH = 2
DEC_BATCH = 32
DEC_SEQ = 1
PAST_LEN = 16384
PAGE_SIZE = 128

HEAD_DIM = 64
N_HEADS = D_MODEL // HEAD_DIM
H_B = N_HEADS // 4
H_A = (N_HEADS - H_B) // 2
H_C = N_HEADS - H_A - H_B
DQK_A = HEAD_DIM // 2
ROW = 2 * HEAD_DIM
H_IDX = 4
D_IDX = 64
ROPE_THETA = 500000.0
ROPE_FRACTION = 4
MOBA_BLOCK = 256
MOBA_TOPK = 3
DSA_TOPK = 256
Q_BLOCK = 128
D_FF = 2688
CONV_W = 3
LN_EPS = 1e-5
NEG_INF = -1e30
DN_ALPHA = (2 * DEPTH) ** 0.25
DN_BETA = (8 * DEPTH) ** -0.25
ADA_SCALE = 0.2
SIZES = (2 * H_A * DQK_A, 2 * H_A * DQK_A, H_A * HEAD_DIM,
         H_B * HEAD_DIM, H_B * HEAD_DIM, H_B * HEAD_DIM,
         H_C * HEAD_DIM, H_C * HEAD_DIM, H_C * HEAD_DIM,
         H_IDX * D_IDX, D_IDX, H_IDX)
D_IN = sum(SIZES)
SPLIT_AT = tuple(int(v) for v in np.cumsum(SIZES)[:-1])

kernel_name = 'hymba_diff_moba_dsa_convffn_decode_step'


def layer_norm(x, g, b):
    xf = x.astype(jnp.float32)
    mu = jnp.mean(xf, axis=-1, keepdims=True)
    var = jnp.mean(jnp.square(xf - mu), axis=-1, keepdims=True)
    y = (xf - mu) * lax.rsqrt(var + LN_EPS)
    return (y * g.astype(jnp.float32) + b.astype(jnp.float32)).astype(x.dtype)


def rope(x, pos):
    rot = x.shape[-1] // ROPE_FRACTION
    half = rot // 2
    inv = jnp.power(ROPE_THETA, -jnp.arange(half, dtype=jnp.float32) / half)
    ang = pos.astype(jnp.float32)[:, None] * inv[None, :]
    cos = jnp.cos(ang)[:, None, :]
    sin = jnp.sin(ang)[:, None, :]
    xf = x.astype(jnp.float32)
    x1 = xf[..., :half]
    x2 = xf[..., half:rot]
    out = jnp.concatenate([x1 * cos - x2 * sin, x2 * cos + x1 * sin, xf[..., rot:]], axis=-1)
    return out.astype(x.dtype)


def ada_modulation(c, w, b):
    m = jax.nn.silu(c) @ w + b
    shift, scale, gate = jnp.split(m, 3, axis=-1)
    return shift[:, None, :], scale[:, None, :], gate[:, None, :]


def project(h, pos, w_in):
    b, t, _ = h.shape
    qa, ka, va, qb, kb, vb, qc, kc, vc, qi, ki, wi = jnp.split(h @ w_in, SPLIT_AT, axis=-1)
    qa = rope(qa.reshape(b, t, 2 * H_A, DQK_A), pos).reshape(b, t, H_A, 2, DQK_A)
    ka = rope(ka.reshape(b, t, 2 * H_A, DQK_A), pos).reshape(b, t, H_A, 2 * DQK_A)
    qb = rope(qb.reshape(b, t, H_B, HEAD_DIM), pos)
    kb = rope(kb.reshape(b, t, H_B, HEAD_DIM), pos)
    qc = rope(qc.reshape(b, t, H_C, HEAD_DIM), pos)
    kc = rope(kc.reshape(b, t, H_C, HEAD_DIM), pos)
    qi = rope(qi.reshape(b, t, H_IDX, D_IDX), pos)
    ki = rope(ki.reshape(b, t, 1, D_IDX), pos).reshape(b, t, D_IDX)
    wi = wi * (H_IDX ** -0.5)
    kv = jnp.concatenate([
        jnp.concatenate([ka, va.reshape(b, t, H_A, HEAD_DIM)], axis=-1),
        jnp.concatenate([kb, vb.reshape(b, t, H_B, HEAD_DIM)], axis=-1),
        jnp.concatenate([kc, vc.reshape(b, t, H_C, HEAD_DIM)], axis=-1)], axis=2)
    return (qa, qb, qc, qi, wi), kv, ki


def diff_attention(q, k, v, lam, q0):
    tb = q.shape[0]
    n = q0 + tb
    s = jnp.einsum('qhcd,khcd->chqk', q, k[:n], preferred_element_type=jnp.float32) * (DQK_A ** -0.5)
    qpos = q0 + jnp.arange(tb)
    mask = jnp.arange(n)[None, :] <= qpos[:, None]
    p = jax.nn.softmax(jnp.where(mask, s, NEG_INF), axis=-1)
    a = p[0] - lam * p[1]
    return jnp.einsum('hqk,khd->qhd', a.astype(v.dtype), v[:n])


def moba_attention(q, k, v, q0):
    tb = q.shape[0]
    scale = HEAD_DIM ** -0.5
    n_past = q0 // MOBA_BLOCK
    ksel = min(MOBA_TOPK, n_past)
    own0 = n_past * MOBA_BLOCK
    qpos = q0 + jnp.arange(tb)
    k_own = k[own0:q0 + tb]
    v_own = v[own0:q0 + tb]
    s_own = jnp.einsum('qhd,khd->hqk', q, k_own, preferred_element_type=jnp.float32) * scale
    own_mask = (own0 + jnp.arange(q0 + tb - own0))[None, :] <= qpos[:, None]
    s_own = jnp.where(own_mask, s_own, NEG_INF)
    if ksel == 0:
        p = jax.nn.softmax(s_own, axis=-1)
        return jnp.einsum('hqk,khd->qhd', p.astype(v.dtype), v_own)
    k_blk = k[:own0].reshape(n_past, MOBA_BLOCK, H_B, HEAD_DIM)
    v_blk = v[:own0].reshape(n_past, MOBA_BLOCK, H_B, HEAD_DIM)
    means = jnp.mean(k_blk.astype(jnp.float32), axis=1)
    gate = jnp.einsum('qhd,nhd->hqn', q.astype(jnp.float32), means)
    _, sel = lax.top_k(gate, ksel)
    hidx = jnp.arange(H_B)[:, None, None]
    k_sel = k_blk.transpose(2, 0, 1, 3)[hidx, sel]
    v_sel = v_blk.transpose(2, 0, 1, 3)[hidx, sel]
    s_sel = jnp.einsum('qhd,hqjkd->hqjk', q, k_sel, preferred_element_type=jnp.float32)
    s_sel = s_sel.reshape(H_B, tb, ksel * MOBA_BLOCK) * scale
    p = jax.nn.softmax(jnp.concatenate([s_sel, s_own], axis=-1), axis=-1)
    p_sel = p[..., :ksel * MOBA_BLOCK].reshape(H_B, tb, ksel, MOBA_BLOCK).astype(v.dtype)
    p_own = p[..., ksel * MOBA_BLOCK:].astype(v.dtype)
    return (jnp.einsum('hqjk,hqjkd->qhd', p_sel, v_sel)
            + jnp.einsum('hqk,khd->qhd', p_own, v_own))


def dsa_attention(q, qi, wi, k, v, ik, q0, topk):
    tb = q.shape[0]
    n = q0 + tb
    kk = min(topk, n)
    qpos = q0 + jnp.arange(tb)
    mask = jnp.arange(n)[None, :] <= qpos[:, None]
    logits = jnp.einsum('qhd,kd->qhk', qi, ik[:n], preferred_element_type=jnp.float32) * (D_IDX ** -0.5)
    score = jnp.einsum('qhk,qh->qk', jax.nn.relu(logits), wi.astype(jnp.float32))
    score = jnp.where(mask, score, NEG_INF)
    _, sel = lax.top_k(score, kk)
    valid = sel <= qpos[:, None]
    k_sel = k[sel]
    v_sel = v[sel]
    s = jnp.einsum('qhd,qkhd->hqk', q, k_sel, preferred_element_type=jnp.float32) * (HEAD_DIM ** -0.5)
    p = jax.nn.softmax(jnp.where(valid[None], s, NEG_INF), axis=-1)
    return jnp.einsum('hqk,qkhd->qhd', p.astype(v.dtype), v_sel)


def mix_sequence(qa, qb, qc, qi, wi, kv, ik, lam):
    n_keys, tq = kv.shape[0], qa.shape[0]
    q_start = n_keys - tq
    topk = min(DSA_TOPK, n_keys // 4)
    ka = kv[:, :H_A, :2 * DQK_A].reshape(n_keys, H_A, 2, DQK_A)
    va = kv[:, :H_A, 2 * DQK_A:]
    kb = kv[:, H_A:H_A + H_B, :HEAD_DIM]
    vb = kv[:, H_A:H_A + H_B, HEAD_DIM:]
    kc = kv[:, H_A + H_B:, :HEAD_DIM]
    vc = kv[:, H_A + H_B:, HEAD_DIM:]
    blk = min(Q_BLOCK, tq)
    outs_a, outs_b, outs_c = [], [], []
    for s in range(0, tq, blk):
        q0 = q_start + s
        outs_a.append(diff_attention(qa[s:s + blk], ka, va, lam, q0))
        outs_b.append(moba_attention(qb[s:s + blk], kb, vb, q0))
        outs_c.append(dsa_attention(qc[s:s + blk], qi[s:s + blk], wi[s:s + blk], kc, vc, ik, q0, topk))
    return (jnp.concatenate(outs_a, axis=0), jnp.concatenate(outs_b, axis=0),
            jnp.concatenate(outs_c, axis=0))


def layer_forward(x, c, pos, lw, lam, lam_init, past):
    b, t, _ = x.shape
    shift, scale, gate = ada_modulation(c, lw['w_ada_attn'], lw['b_ada_attn'])
    h = x * (1 + scale) + shift
    (qa, qb, qc, qi, wi), kv_new, ik_new = project(h, pos, lw['w_in'])
    if past is None:
        def per_seq(args):
            qa_s, qb_s, qc_s, qi_s, wi_s, kv_s, ik_s = args
            return mix_sequence(qa_s, qb_s, qc_s, qi_s, wi_s, kv_s, ik_s, lam)
        o_a, o_b, o_c = lax.map(per_seq, (qa, qb, qc, qi, wi, kv_new, ik_new))
        conv_hist = jnp.zeros((b, CONV_W - 1, D_FF), x.dtype)
    else:
        cache_kv, cache_index_k, page_table, conv_hist, l = past
        def per_seq(args):
            qa_s, qb_s, qc_s, qi_s, wi_s, pt_s, kvn_s, ikn_s = args
            kv_s = jnp.concatenate([cache_kv[l, pt_s].reshape(-1, N_HEADS, ROW), kvn_s], axis=0)
            ik_s = jnp.concatenate([cache_index_k[l, pt_s].reshape(-1, D_IDX), ikn_s], axis=0)
            return mix_sequence(qa_s, qb_s, qc_s, qi_s, wi_s, kv_s, ik_s, lam)
        o_a, o_b, o_c = lax.map(per_seq, (qa, qb, qc, qi, wi, page_table, kv_new, ik_new))
    oa = o_a.astype(jnp.float32)
    oa = oa * lax.rsqrt(jnp.mean(jnp.square(oa), axis=-1, keepdims=True) + LN_EPS)
    oa = oa * lw['diff_norm_g'].astype(jnp.float32) * (1.0 - lam_init)
    o = jnp.concatenate([oa.astype(x.dtype).reshape(b, t, -1), o_b.reshape(b, t, -1),
                         o_c.reshape(b, t, -1)], axis=-1)
    x = layer_norm(DN_ALPHA * x + (1 + gate) * (o @ lw['w_out']), lw['ln1_g'], lw['ln1_b'])
    shift, scale, gate = ada_modulation(c, lw['w_ada_ffn'], lw['b_ada_ffn'])
    h = x * (1 + scale) + shift
    u = h @ lw['w_up']
    g = h @ lw['w_gate']
    hist = jnp.concatenate([conv_hist.astype(g.dtype), g], axis=1)
    cw = lw['conv_w']
    gc = lw['conv_b'] + hist[:, 0:t] * cw[0] + hist[:, 1:1 + t] * cw[1] + hist[:, 2:2 + t] * cw[2]
    f = (jax.nn.silu(gc) * u) @ lw['w_down']
    x = layer_norm(DN_ALPHA * x + (1 + gate) * f, lw['ln2_g'], lw['ln2_b'])
    return x, kv_new, ik_new, hist[:, t:]


def setup_inputs(seed: int = 0) -> dict:
    key = jax.random.key(seed)
    ks = jax.random.split(key, 30)
    f32 = jnp.float32
    n_pages = PAST_LEN // PAGE_SIZE
    n_used = DEC_BATCH * n_pages
    n_pool = n_used + max(1, n_used // 4)

    def nrm(k, shape, s=1.0):
        return jax.random.normal(k, shape, f32) * s

    page_table = jax.random.permutation(ks[4], n_pool)[:n_used].reshape(DEC_BATCH, n_pages).astype(jnp.int32)
    d = D_MODEL
    return {
        'x_prompt': nrm(ks[0], (BATCH, SEQ, d)),
        'x_sample': nrm(ks[1], (DEC_BATCH, DEC_SEQ, d)),
        'cache_kv': nrm(ks[2], (DEPTH, n_pool, PAGE_SIZE, N_HEADS, ROW)),
        'cache_index_k': nrm(ks[3], (DEPTH, n_pool, PAGE_SIZE, D_IDX)),
        'page_table': page_table,
        'state_ffn_conv': nrm(ks[5], (DEPTH, DEC_BATCH, CONV_W - 1, D_FF)),
        'c_prompt': nrm(ks[6], (BATCH, d)),
        'c_sample': nrm(ks[7], (DEC_BATCH, d)),
        'w_in': nrm(ks[8], (DEPTH, d, D_IN), d ** -0.5),
        'w_out': nrm(ks[9], (DEPTH, d, d), DN_BETA * d ** -0.5),
        'diff_lambda': nrm(ks[10], (DEPTH, 4, DQK_A), 0.1),
        'diff_norm_g': 1.0 + nrm(ks[11], (DEPTH, HEAD_DIM), 0.02),
        'w_ada_attn': nrm(ks[12], (DEPTH, d, 3 * d), ADA_SCALE * d ** -0.5),
        'b_ada_attn': nrm(ks[13], (DEPTH, 3 * d), 0.02),
        'ln1_g': 1.0 + nrm(ks[14], (DEPTH, d), 0.02),
        'ln1_b': nrm(ks[15], (DEPTH, d), 0.02),
        'w_ada_ffn': nrm(ks[16], (DEPTH, d, 3 * d), ADA_SCALE * d ** -0.5),
        'b_ada_ffn': nrm(ks[17], (DEPTH, 3 * d), 0.02),
        'w_up': nrm(ks[18], (DEPTH, d, D_FF), d ** -0.5),
        'w_gate': nrm(ks[19], (DEPTH, d, D_FF), d ** -0.5),
        'conv_w': nrm(ks[20], (DEPTH, CONV_W, D_FF), CONV_W ** -0.5),
        'conv_b': nrm(ks[21], (DEPTH, D_FF), 0.02),
        'w_down': nrm(ks[22], (DEPTH, D_FF, d), DN_BETA * D_FF ** -0.5),
        'ln2_g': 1.0 + nrm(ks[23], (DEPTH, d), 0.02),
        'ln2_b': nrm(ks[24], (DEPTH, d), 0.02),
    }


def reference(x_prompt, x_sample, cache_kv, cache_index_k, page_table, state_ffn_conv, c_prompt, c_sample,
              w_in, w_out, diff_lambda, diff_norm_g, w_ada_attn, b_ada_attn, ln1_g, ln1_b,
              w_ada_ffn, b_ada_ffn, w_up, w_gate, conv_w, conv_b, w_down, ln2_g, ln2_b):
    pos_p = jnp.arange(x_prompt.shape[1], dtype=jnp.int32)
    pos_s = PAST_LEN + jnp.arange(x_sample.shape[1], dtype=jnp.int32)
    xp, xs = x_prompt, x_sample
    kvp, kvs, ikp, iks, cvp, cvs = [], [], [], [], [], []
    for l in range(DEPTH):
        lw = {'w_in': w_in[l], 'w_out': w_out[l], 'diff_norm_g': diff_norm_g[l],
              'w_ada_attn': w_ada_attn[l], 'b_ada_attn': b_ada_attn[l], 'ln1_g': ln1_g[l], 'ln1_b': ln1_b[l],
              'w_ada_ffn': w_ada_ffn[l], 'b_ada_ffn': b_ada_ffn[l], 'w_up': w_up[l], 'w_gate': w_gate[l],
              'conv_w': conv_w[l], 'conv_b': conv_b[l], 'w_down': w_down[l], 'ln2_g': ln2_g[l], 'ln2_b': ln2_b[l]}
        dl = diff_lambda[l].astype(jnp.float32)
        lam_init = 0.8 - 0.6 * math.exp(-0.3 * l)
        lam = jnp.exp(jnp.sum(dl[0] * dl[1])) - jnp.exp(jnp.sum(dl[2] * dl[3])) + lam_init
        xp, kv, ik, cv = layer_forward(xp, c_prompt, pos_p, lw, lam, lam_init, None)
        kvp.append(kv)
        ikp.append(ik)
        cvp.append(cv)
        xs, kv, ik, cv = layer_forward(xs, c_sample, pos_s, lw, lam, lam_init,
                                       (cache_kv, cache_index_k, page_table, state_ffn_conv[l], l))
        kvs.append(kv)
        iks.append(ik)
        cvs.append(cv)
    return (xp, xs, jnp.stack(kvp), jnp.stack(kvs), jnp.stack(ikp), jnp.stack(iks), jnp.stack(cvp), jnp.stack(cvs))
```

```python
import functools
import math

import jax
import jax.numpy as jnp
import numpy as np
from jax import lax
from jax.experimental import pallas as pl
from jax.experimental.pallas import tpu as pltpu

F32 = jnp.float32
BF16 = jnp.bfloat16

HEAD_DIM = 64
N_HEADS = 16
H_B = 4
H_A = 6
H_C = 6
DQK_A = 32
ROW = 2 * HEAD_DIM
H_IDX = 4
D_IDX = 64
ROPE_THETA = 500000.0
MOBA_BLOCK = 256
MOBA_TOPK = 3
DSA_TOPK = 256
Q_BLOCK = 128
LN_EPS = 1e-5
NEG_INF = -1e30
M_INIT = -1e20
INT_MIN = -2 ** 31

OFF_QA, OFF_KA, OFF_VA = 0, 384, 768
OFF_QB, OFF_KB, OFF_VB = 1152, 1408, 1664
OFF_QC, OFF_KC, OFF_VC = 1920, 2304, 2688
OFF_QI, OFF_KI, OFF_WI = 3072, 3328, 3392
D_IN = 3396

LANE = 128
QW = 1280
Q_QA, Q_QC, Q_QB, Q_QI = 0, 384, 768, 1024
KVW = N_HEADS * ROW
TAILW = 128
WP = QW + KVW + TAILW
VMEM_LIMIT = 56 * 1024 * 1024

PAT_QA, PAT_QB, PAT_KA, PAT_KB, PAT_TAIL = 0, 1, 2, 3, 4
PAT_HALF = (4, 8, 4, 8, 8)


def _perm_columns():
    idx = []
    idx += list(range(OFF_QA, OFF_QA + 384))
    idx += list(range(OFF_QC, OFF_QC + 384))
    idx += list(range(OFF_QB, OFF_QB + 256))
    idx += list(range(OFF_QI, OFF_QI + 256))
    for h in range(H_A):
        idx += list(range(OFF_KA + h * 64, OFF_KA + (h + 1) * 64))
        idx += list(range(OFF_VA + h * 64, OFF_VA + (h + 1) * 64))
    for h in range(H_B):
        idx += list(range(OFF_KB + h * 64, OFF_KB + (h + 1) * 64))
        idx += list(range(OFF_VB + h * 64, OFF_VB + (h + 1) * 64))
    for h in range(H_C):
        idx += list(range(OFF_KC + h * 64, OFF_KC + (h + 1) * 64))
        idx += list(range(OFF_VC + h * 64, OFF_VC + (h + 1) * 64))
    idx += list(range(OFF_KI, OFF_KI + 64))
    idx += list(range(OFF_WI, OFF_WI + 4))
    idx += [D_IN] * 60
    return np.asarray(idx, np.int32)


_PERM = _perm_columns()
_GROUP_PAT = ([PAT_QA] * 3 + [PAT_QB] * 7 + [PAT_KA] * H_A + [PAT_KB] * (H_B + H_C) + [PAT_TAIL])


def _decode_q_index():
    idx = np.full((32, KVW), QW, np.int32)
    for h in range(H_A):
        for c in range(2):
            idx[2 * h + c, h * ROW + c * 32: h * ROW + c * 32 + 32] = np.arange(32) + Q_QA + h * 64 + c * 32
    for h in range(H_B):
        idx[16 + h, (H_A + h) * ROW: (H_A + h) * ROW + 64] = np.arange(64) + Q_QB + h * 64
    for h in range(H_C):
        idx[24 + h, (H_A + H_B + h) * ROW: (H_A + H_B + h) * ROW + 64] = np.arange(64) + Q_QC + h * 64
    return idx.reshape(-1)


_DEC_QIDX = _decode_q_index()
DEC_ROWS = 32


def _split(x):
    hi = x.astype(BF16)
    lo = (x - hi.astype(F32)).astype(BF16)
    return hi, lo


_NN = (((1,), (0,)), ((), ()))
_NT = (((1,), (1,)), ((), ()))


def _dot(a, b, dims=_NN):
    return lax.dot_general(a, b, dims, preferred_element_type=F32)


def _dot3(a, b, dims=_NN):
    ah, al = _split(a)
    bh, bl = _split(b)
    return _dot(ah, bh, dims) + (_dot(ah, bl, dims) + _dot(al, bh, dims))


def _layer_norm(x, g, b):
    mu = jnp.mean(x, axis=-1, keepdims=True)
    xc = x - mu
    var = jnp.mean(xc * xc, axis=-1, keepdims=True)
    return xc * lax.rsqrt(var + LN_EPS) * g + b


def _cparams(sem, vmem=None):
    return pltpu.CompilerParams(dimension_semantics=sem, vmem_limit_bytes=vmem)


def _ada_kernel(c_ref, w_ref, b_ref, o_ref):
    c = c_ref[...]
    a = c / (1.0 + jnp.exp(-c))
    o_ref[...] = _dot3(a, w_ref[...]) + b_ref[...]


def _ada_call(c_all, w, b):
    depth, d, d3 = w.shape
    n = c_all.shape[0]
    return pl.pallas_call(
        _ada_kernel,
        grid=(depth, d3 // d),
        in_specs=[pl.BlockSpec((n, d), lambda l, j: (0, 0)),
                  pl.BlockSpec((None, d, d), lambda l, j: (l, 0, j)),
                  pl.BlockSpec((None, 1, d), lambda l, j: (l, 0, j))],
        out_specs=pl.BlockSpec((None, n, d), lambda l, j: (l, 0, j)),
        out_shape=jax.ShapeDtypeStruct((depth, n, d3), F32),
        compiler_params=_cparams(("arbitrary", "arbitrary")),
        name="ada_modulation",
    )(c_all, w, b.reshape(depth, 1, d3))


def _rope_tables(pos):
    posf = pos.astype(F32)[:, None]
    r = pos.shape[0]

    def family(half, group):
        inv = jnp.power(ROPE_THETA, -jnp.arange(half, dtype=F32) / half)
        ang = posf * inv[None, :]
        cos, sin = jnp.cos(ang), jnp.sin(ang)
        rest = group - 2 * half
        cg = jnp.concatenate([cos, cos, jnp.ones((r, rest), F32)], axis=-1)
        sg = jnp.concatenate([-sin, sin, jnp.zeros((r, rest), F32)], axis=-1)
        return cg, sg

    ca, sa = family(4, 32)
    cb, sb = family(8, 64)
    one64, zero64 = jnp.ones((r, 64), F32), jnp.zeros((r, 64), F32)
    tail_c = jnp.concatenate([jnp.full((r, 4), H_IDX ** -0.5, F32), jnp.ones((r, 60), F32)], axis=-1)
    tabs = [jnp.tile(ca, (1, 4)), jnp.tile(sa, (1, 4)),
            jnp.tile(cb, (1, 2)), jnp.tile(sb, (1, 2)),
            jnp.concatenate([ca, ca, one64], -1), jnp.concatenate([sa, sa, zero64], -1),
            jnp.concatenate([cb, one64], -1), jnp.concatenate([sb, zero64], -1),
            jnp.concatenate([cb, tail_c], -1), jnp.concatenate([sb, zero64], -1)]
    return jnp.stack(tabs)


def _proj_kernel(x_ref, shift_ref, scale_ref, w_ref, tab_ref,
                 q_ref, kv_ref, kva_ref, kvb_ref, kvc_ref, tail_ref):
    h = x_ref[...] * (1.0 + scale_ref[...]) + shift_ref[...]
    hb = h.astype(BF16)
    lane = lax.broadcasted_iota(jnp.int32, (1, LANE), 1)
    x1_mask = {4: (lane % 32) < 4, 8: (lane % 64) < 8}

    def rope(y, g):
        pat = _GROUP_PAT[g]
        half = PAT_HALF[pat]
        partner = jnp.where(x1_mask[half], pltpu.roll(y, LANE - half, 1), pltpu.roll(y, half, 1))
        return y * tab_ref[2 * pat] + partner * tab_ref[2 * pat + 1]

    n_groups = WP // LANE
    g = 0
    while g < n_groups:
        width = min(2, n_groups - g)
        y = _dot(hb, w_ref[:, g * LANE:(g + width) * LANE])
        for k in range(width):
            gg = g + k
            r = rope(y[:, k * LANE:(k + 1) * LANE], gg)
            c0 = gg * LANE
            if c0 < QW:
                q_ref[:, c0:c0 + LANE] = r
            elif c0 < QW + KVW:
                c = c0 - QW
                kv_ref[:, c:c + LANE] = r
                rb = r.astype(BF16)
                head = c // ROW
                if head < H_A:
                    kva_ref[:, c:c + LANE] = rb
                elif head < H_A + H_B:
                    kvb_ref[:, c - H_A * ROW:c - H_A * ROW + LANE] = rb
                else:
                    cc = c - (H_A + H_B) * ROW
                    kvc_ref[:, cc:cc + LANE] = rb
            else:
                tail_ref[...] = r
        g += width


def _proj_call(x, shift, scale, wp, tabs, tm):
    g_n, tg, d = x.shape
    rm = shift.shape[1]
    nt = tg // tm
    if rm == 1:
        mod_spec = pl.BlockSpec((None, 1, d), lambda i, g: (g, 0, 0))
    else:
        mod_spec = pl.BlockSpec((None, tm, d), lambda i, g: (g, i, 0))

    def row_spec(w):
        return pl.BlockSpec((None, tm, w), lambda i, g: (g, i, 0))

    widths = (QW, KVW, H_A * ROW, H_B * ROW, H_C * ROW, TAILW)
    dtypes = (F32, F32, BF16, BF16, BF16, F32)
    return pl.pallas_call(
        _proj_kernel,
        grid=(nt, g_n),
        in_specs=[row_spec(d), mod_spec, mod_spec,
                  pl.BlockSpec((d, WP), lambda i, g: (0, 0)),
                  pl.BlockSpec((10, tm, LANE), lambda i, g: (0, i, 0))],
        out_specs=[row_spec(w) for w in widths],
        out_shape=[jax.ShapeDtypeStruct((g_n, tg, w), dt) for w, dt in zip(widths, dtypes)],
        compiler_params=_cparams(("arbitrary", "arbitrary"), VMEM_LIMIT),
        name="in_proj_rope",
    )(x, shift, scale, wp, tabs)


def _flash_step(carry, s, v):
    m, l, acc = carry
    m_new = jnp.maximum(m, jnp.max(s, axis=-1, keepdims=True))
    alpha = jnp.exp(m - m_new)
    p = jnp.exp(s - m_new)
    l = alpha * l + jnp.sum(p, axis=-1, keepdims=True)
    acc = alpha * acc + _dot(p.astype(BF16), v)
    return m_new, l, acc


def _flash_init(rows, dv):
    return (jnp.full((rows, 1), M_INIT, F32), jnp.zeros((rows, 1), F32), jnp.zeros((rows, dv), F32))


def _causal_bias(q0, k0, rows, cols):
    qpos = q0 + lax.broadcasted_iota(jnp.int32, (rows, cols), 0)
    kpos = k0 + lax.broadcasted_iota(jnp.int32, (rows, cols), 1)
    return jnp.where(kpos <= qpos, 0.0, NEG_INF).astype(F32)


def _diff_kernel(lam_ref, q_ref, kv_ref, g_ref, o_ref, *, tq, norm_scale):
    i = pl.program_id(1)
    q0 = i * tq
    lam = lam_ref[0]
    diag_bias = _causal_bias(0, 0, tq, tq)
    for h in range(H_A):
        v_lo = h * ROW + 2 * DQK_A
        outs = []
        for c in range(2):
            k_lo = h * ROW + c * DQK_A
            qh = (q_ref[:, h * 64 + c * 32:h * 64 + c * 32 + 32] * (DQK_A ** -0.5)).astype(BF16)

            def body(j, carry, k_lo=k_lo, v_lo=v_lo, qh=qh):
                r0 = pl.multiple_of(j * tq, tq)
                k = kv_ref[pl.ds(r0, tq), k_lo:k_lo + DQK_A]
                v = kv_ref[pl.ds(r0, tq), v_lo:v_lo + HEAD_DIM]
                return _flash_step(carry, _dot(qh, k, _NT), v)

            carry = lax.fori_loop(0, i, body, _flash_init(tq, HEAD_DIM))
            r0 = pl.multiple_of(q0, tq)
            k = kv_ref[pl.ds(r0, tq), k_lo:k_lo + DQK_A]
            v = kv_ref[pl.ds(r0, tq), v_lo:v_lo + HEAD_DIM]
            m, l, acc = _flash_step(carry, _dot(qh, k, _NT) + diag_bias, v)
            outs.append(acc / l)
        o = outs[0] - lam * outs[1]
        o = o * lax.rsqrt(jnp.mean(o * o, axis=-1, keepdims=True) + LN_EPS)
        o = o * g_ref[...] * norm_scale
        o_ref[:, h * HEAD_DIM:(h + 1) * HEAD_DIM] = o.astype(o_ref.dtype)


def _diff_call(lam, q, kva, g, tq, norm_scale):
    b, t, _ = q.shape
    return pl.pallas_call(
        functools.partial(_diff_kernel, tq=tq, norm_scale=norm_scale),
        grid=(b, t // tq),
        in_specs=[pl.BlockSpec(memory_space=pltpu.SMEM),
                  pl.BlockSpec((None, tq, 384), lambda b_, i: (b_, i, Q_QA // 384)),
                  pl.BlockSpec((None, t, H_A * ROW), lambda b_, i: (b_, 0, 0)),
                  pl.BlockSpec((1, HEAD_DIM), lambda b_, i: (0, 0))],
        out_specs=pl.BlockSpec((None, tq, H_A * HEAD_DIM), lambda b_, i: (b_, i, 0)),
        out_shape=jax.ShapeDtypeStruct((b, t, H_A * HEAD_DIM), BF16),
        compiler_params=_cparams(("arbitrary", "arbitrary"), VMEM_LIMIT),
        name="diff_attention",
    )(lam, q, kva, g)


def _moba_kernel(q_ref, kv_ref, o_ref, mean_ref, *, n_blocks):
    i = pl.program_id(1)
    tq = MOBA_BLOCK

    @pl.when(i == 0)
    def _():
        mean_ref[...] = jnp.zeros_like(mean_ref)
        for blk in range(n_blocks):
            kblk = kv_ref[blk * tq:(blk + 1) * tq, :].astype(F32)
            mean_ref[blk:blk + 1, :] = jnp.mean(kblk, axis=0, keepdims=True)

    ksel = jnp.minimum(MOBA_TOPK, i)
    lane = lax.broadcasted_iota(jnp.int32, (tq, LANE), 1)
    diag_bias = _causal_bias(0, 0, tq, tq)
    for h in range(H_B):
        qf = q_ref[:, h * HEAD_DIM:(h + 1) * HEAD_DIM]
        qh = (qf * (HEAD_DIM ** -0.5)).astype(BF16)
        gate = _dot3(qf, mean_ref[:, h * ROW:h * ROW + HEAD_DIM], _NT)
        k_lo, v_lo = h * ROW, h * ROW + HEAD_DIM

        def body(c, carry, gate=gate, qh=qh, k_lo=k_lo, v_lo=v_lo):
            col = jnp.sum(jnp.where(lane == c, gate, 0.0), axis=-1, keepdims=True)
            beats = ((gate > col) | ((gate == col) & (lane < c))) & (lane < i)
            rank = jnp.sum(beats.astype(jnp.int32), axis=-1, keepdims=True)
            bias = jnp.where(rank < ksel, 0.0, NEG_INF).astype(F32)
            r0 = pl.multiple_of(c * tq, tq)
            k = kv_ref[pl.ds(r0, tq), k_lo:k_lo + HEAD_DIM]
            v = kv_ref[pl.ds(r0, tq), v_lo:v_lo + HEAD_DIM]
            return _flash_step(carry, _dot(qh, k, _NT) + bias, v)

        carry = lax.fori_loop(0, i, body, _flash_init(tq, HEAD_DIM))
        r0 = pl.multiple_of(i * tq, tq)
        k = kv_ref[pl.ds(r0, tq), k_lo:k_lo + HEAD_DIM]
        v = kv_ref[pl.ds(r0, tq), v_lo:v_lo + HEAD_DIM]
        m, l, acc = _flash_step(carry, _dot(qh, k, _NT) + diag_bias, v)
        o_ref[:, h * HEAD_DIM:(h + 1) * HEAD_DIM] = (acc / l).astype(o_ref.dtype)


def _moba_call(q, kvb):
    b, t, _ = q.shape
    tq = MOBA_BLOCK
    n_blocks = t // tq
    assert n_blocks <= LANE
    return pl.pallas_call(
        functools.partial(_moba_kernel, n_blocks=n_blocks),
        grid=(b, n_blocks),
        in_specs=[pl.BlockSpec((None, tq, 256), lambda b_, i: (b_, i, Q_QB // 256)),
                  pl.BlockSpec((None, t, H_B * ROW), lambda b_, i: (b_, 0, 0))],
        out_specs=pl.BlockSpec((None, tq, H_B * HEAD_DIM), lambda b_, i: (b_, i, 0)),
        out_shape=jax.ShapeDtypeStruct((b, t, H_B * HEAD_DIM), BF16),
        scratch_shapes=[pltpu.VMEM((LANE, H_B * ROW), F32)],
        compiler_params=_cparams(("arbitrary", "arbitrary"), VMEM_LIMIT),
        name="moba_attention",
    )(q, kvb)


def _sort_key(score):
    bits = lax.bitcast_convert_type(jnp.where(score == 0.0, 0.0, score), jnp.int32)
    return jnp.where(bits < 0, bits ^ jnp.int32(0x7FFFFFFF), bits)


def _kth_largest_key(count_ge, kk, rows):
    def body(it, prefix):
        bit = lax.shift_left(jnp.int32(1), 31 - it)
        cand_u = prefix | bit
        cnt = count_ge(cand_u ^ jnp.int32(INT_MIN))
        return jnp.where(cnt >= kk, cand_u, prefix)

    prefix = lax.fori_loop(0, 32, body, jnp.zeros((rows, 1), jnp.int32))
    return prefix ^ jnp.int32(INT_MIN)


def _dsa_kernel(q_ref, qi_ref, tq_ref, tk_ref, kv_ref, o_ref, key_ref, bias_ref, thr_ref, need_ref,
                *, tq, topk):
    i = pl.program_id(1)
    q0 = i * tq
    n_chunks = i + 1
    rows = lax.broadcasted_iota(jnp.int32, (tq, 1), 0)
    qpos = q0 + rows
    kk = jnp.minimum(topk, ((qpos // Q_BLOCK) + 1) * Q_BLOCK)
    diag_bias = _causal_bias(0, 0, tq, tq)

    def score_chunk(j, masked):
        r0 = pl.multiple_of(j * tq, tq)
        ik = tk_ref[pl.ds(r0, tq), 0:D_IDX]
        score = jnp.zeros((tq, tq), F32)
        for h in range(H_IDX):
            logits = _dot3(qi_ref[:, h * D_IDX:(h + 1) * D_IDX], ik, _NT) * (D_IDX ** -0.5)
            score = score + jnp.maximum(logits, 0.0) * tq_ref[:, D_IDX + h:D_IDX + h + 1]
        if masked:
            score = jnp.where(diag_bias < 0.0, NEG_INF, score)
        key_ref[j] = _sort_key(score)

    def score_body(j, _):
        score_chunk(j, False)
        return 0

    lax.fori_loop(0, i, score_body, 0)
    score_chunk(i, True)

    def count(pred):
        def body(j, cnt):
            return cnt + pred(key_ref[j]).astype(jnp.int32)
        cnt = lax.fori_loop(0, n_chunks, body, jnp.zeros((tq, tq), jnp.int32))
        return jnp.sum(cnt, axis=-1, keepdims=True)

    @pl.when(n_chunks * tq <= topk)
    def _():
        thr_ref[...] = jnp.full((tq, LANE), INT_MIN, jnp.int32)
        need_ref[...] = jnp.full((tq, LANE), 2 ** 30, jnp.int32)

    @pl.when(n_chunks * tq > topk)
    def _():
        thr = _kth_largest_key(lambda cand: count(lambda k: k >= cand), kk, tq)
        n_gt = count(lambda k: k > thr)
        thr_ref[...] = jnp.broadcast_to(thr, (tq, LANE))
        need_ref[...] = jnp.broadcast_to(kk - n_gt, (tq, LANE))

    thr = thr_ref[:, 0:1]
    need = need_ref[:, 0:1].astype(F32)
    tri = (lax.broadcasted_iota(jnp.int32, (tq, tq), 0)
           <= lax.broadcasted_iota(jnp.int32, (tq, tq), 1)).astype(BF16)

    def bias_body(j, seen):
        key = key_ref[j]
        eq = key == thr
        cum = seen + _dot(eq.astype(BF16), tri)
        sel = (key > thr) | (eq & (cum <= need))
        bias_ref[j] = jnp.where(sel, 0.0, NEG_INF).astype(F32)
        return seen + jnp.sum(eq.astype(F32), axis=-1, keepdims=True)

    lax.fori_loop(0, n_chunks, bias_body, jnp.zeros((tq, 1), F32))

    for h in range(H_C):
        qh = (q_ref[:, h * HEAD_DIM:(h + 1) * HEAD_DIM] * (HEAD_DIM ** -0.5)).astype(BF16)
        k_lo, v_lo = h * ROW, h * ROW + HEAD_DIM

        def step(j, carry, extra, qh=qh, k_lo=k_lo, v_lo=v_lo):
            r0 = pl.multiple_of(j * tq, tq)
            k = kv_ref[pl.ds(r0, tq), k_lo:k_lo + HEAD_DIM]
            v = kv_ref[pl.ds(r0, tq), v_lo:v_lo + HEAD_DIM]
            s = _dot(qh, k, _NT) + bias_ref[j]
            if extra is not None:
                s = s + extra
            return _flash_step(carry, s, v)

        carry = lax.fori_loop(0, i, lambda j, c: step(j, c, None), _flash_init(tq, HEAD_DIM))
        m, l, acc = step(i, carry, diag_bias)
        o_ref[:, h * HEAD_DIM:(h + 1) * HEAD_DIM] = (acc / l).astype(o_ref.dtype)


def _dsa_call(q, tail, kvc, tq, topk):
    b, t, _ = q.shape
    return pl.pallas_call(
        functools.partial(_dsa_kernel, tq=tq, topk=topk),
        grid=(b, t // tq),
        in_specs=[pl.BlockSpec((None, tq, 384), lambda b_, i: (b_, i, Q_QC // 384)),
                  pl.BlockSpec((None, tq, 256), lambda b_, i: (b_, i, Q_QI // 256)),
                  pl.BlockSpec((None, tq, TAILW), lambda b_, i: (b_, i, 0)),
                  pl.BlockSpec((None, t, TAILW), lambda b_, i: (b_, 0, 0)),
                  pl.BlockSpec((None, t, H_C * ROW), lambda b_, i: (b_, 0, 0))],
        out_specs=pl.BlockSpec((None, tq, H_C * HEAD_DIM), lambda b_, i: (b_, i, 0)),
        out_shape=jax.ShapeDtypeStruct((b, t, H_C * HEAD_DIM), BF16),
        scratch_shapes=[pltpu.VMEM((t // tq, tq, tq), jnp.int32), pltpu.VMEM((t // tq, tq, tq), F32),
                        pltpu.VMEM((tq, LANE), jnp.int32), pltpu.VMEM((tq, LANE), jnp.int32)],
        compiler_params=_cparams(("arbitrary", "arbitrary"), VMEM_LIMIT),
        name="dsa_attention",
    )(q, q, tail, tail, kvc)


def _out_kernel(oa_ref, ob_ref, oc_ref, x_ref, gate_ref, w_ref, g_ref, b_ref, y_ref, *, alpha):
    wa, wb = H_A * HEAD_DIM, (H_A + H_B) * HEAD_DIM
    o = (_dot(oa_ref[...], w_ref[0:wa, :]) + _dot(ob_ref[...], w_ref[wa:wb, :])
         + _dot(oc_ref[...], w_ref[wb:, :]))
    y = alpha * x_ref[...] + (1.0 + gate_ref[...]) * o
    y_ref[...] = _layer_norm(y, g_ref[...], b_ref[...])


def _out_call(oa, ob, oc, x, gate, w_out, g, b, tm, alpha):
    g_n, tg, d = x.shape
    rm = gate.shape[1]
    if rm == 1:
        mod_spec = pl.BlockSpec((None, 1, d), lambda g_, i: (g_, 0, 0))
    else:
        mod_spec = pl.BlockSpec((None, tm, d), lambda g_, i: (g_, i, 0))

    def row_spec(w):
        return pl.BlockSpec((None, tm, w), lambda g_, i: (g_, i, 0))

    vec = pl.BlockSpec((1, d), lambda g_, i: (0, 0))
    return pl.pallas_call(
        functools.partial(_out_kernel, alpha=alpha),
        grid=(g_n, tg // tm),
        in_specs=[row_spec(oa.shape[-1]), row_spec(ob.shape[-1]), row_spec(oc.shape[-1]), row_spec(d),
                  mod_spec, pl.BlockSpec((d, d), lambda g_, i: (0, 0)), vec, vec],
        out_specs=row_spec(d),
        out_shape=jax.ShapeDtypeStruct((g_n, tg, d), F32),
        compiler_params=_cparams(("arbitrary", "arbitrary"), VMEM_LIMIT),
        name="out_proj_ln",
    )(oa, ob, oc, x, gate, w_out, g.reshape(1, d), b.reshape(1, d))


def _ffn_tail(x, gate, u, gc, wd_ref, g_ref, b_ref, alpha):
    act = gc / (1.0 + jnp.exp(-gc)) * u
    f = _dot(act.astype(BF16), wd_ref[...])
    return _layer_norm(alpha * x + (1.0 + gate) * f, g_ref[...], b_ref[...])


def _ffn_seq_kernel(x_ref, shift_ref, scale_ref, gate_ref, wu_ref, wg_ref, cw_ref, cb_ref, wd_ref,
                    g_ref, b_ref, y_ref, conv_ref, carry_ref, *, alpha, tm):
    j = pl.program_id(1)

    @pl.when(j == 0)
    def _():
        carry_ref[...] = jnp.zeros_like(carry_ref)

    x = x_ref[...]
    hb = (x * (1.0 + scale_ref[...]) + shift_ref[...]).astype(BF16)
    u = _dot(hb, wu_ref[...])
    g = _dot(hb, wg_ref[...])
    row = lax.broadcasted_iota(jnp.int32, (tm, 1), 0)
    prev1, prev2 = carry_ref[1:2, :], carry_ref[0:1, :]
    g1 = jnp.where(row == 0, prev1, pltpu.roll(g, 1, 0))
    g2 = jnp.where(row == 0, prev2, jnp.where(row == 1, prev1, pltpu.roll(g, 2, 0)))
    gc = cb_ref[...] + g2 * cw_ref[0:1, :] + g1 * cw_ref[1:2, :] + g * cw_ref[2:3, :]
    last2 = g[tm - 2:tm, :]
    carry_ref[...] = last2
    conv_ref[...] = last2
    y_ref[...] = _ffn_tail(x, gate_ref[...], u, gc, wd_ref, g_ref, b_ref, alpha)


def _ffn_rows_kernel(x_ref, shift_ref, scale_ref, gate_ref, h0_ref, h1_ref, wu_ref, wg_ref, cw_ref, cb_ref,
                     wd_ref, g_ref, b_ref, y_ref, gout_ref, *, alpha):
    x = x_ref[...]
    hb = (x * (1.0 + scale_ref[...]) + shift_ref[...]).astype(BF16)
    u = _dot(hb, wu_ref[...])
    g = _dot(hb, wg_ref[...])
    gc = cb_ref[...] + h0_ref[...] * cw_ref[0:1, :] + h1_ref[...] * cw_ref[1:2, :] + g * cw_ref[2:3, :]
    gout_ref[...] = g
    y_ref[...] = _ffn_tail(x, gate_ref[...], u, gc, wd_ref, g_ref, b_ref, alpha)


def _ffn_seq_call(x, shift, scale, gate, wu, wg, cw, cb, wd, g, b, tm, alpha):
    g_n, tg, d = x.shape
    dff = wu.shape[1]
    mod_spec = pl.BlockSpec((None, 1, d), lambda g_, i: (g_, 0, 0))
    row_spec = pl.BlockSpec((None, tm, d), lambda g_, i: (g_, i, 0))
    const = lambda shape: pl.BlockSpec(shape, lambda g_, i: (0, 0))
    return pl.pallas_call(
        functools.partial(_ffn_seq_kernel, alpha=alpha, tm=tm),
        grid=(g_n, tg // tm),
        in_specs=[row_spec, mod_spec, mod_spec, mod_spec, const((d, dff)), const((d, dff)),
                  const((3, dff)), const((1, dff)), const((dff, d)), const((1, d)), const((1, d))],
        out_specs=[row_spec, pl.BlockSpec((None, 2, dff), lambda g_, i: (g_, 0, 0))],
        out_shape=[jax.ShapeDtypeStruct((g_n, tg, d), F32), jax.ShapeDtypeStruct((g_n, 2, dff), F32)],
        scratch_shapes=[pltpu.VMEM((2, dff), F32)],
        compiler_params=_cparams(("arbitrary", "arbitrary"), VMEM_LIMIT),
        name="conv_ffn_seq",
    )(x, shift, scale, gate, wu, wg, cw, cb.reshape(1, dff), wd, g.reshape(1, d), b.reshape(1, d))


def _ffn_rows_call(x, shift, scale, gate, h0, h1, wu, wg, cw, cb, wd, g, b, alpha):
    n, d = x.shape
    dff = wu.shape[1]
    full = lambda shape: pl.BlockSpec(shape, lambda i: (0, 0))
    return pl.pallas_call(
        functools.partial(_ffn_rows_kernel, alpha=alpha),
        grid=(1,),
        in_specs=[full((n, d)), full((n, d)), full((n, d)), full((n, d)), full((n, dff)), full((n, dff)),
                  full((d, dff)), full((d, dff)), full((3, dff)), full((1, dff)), full((dff, d)),
                  full((1, d)), full((1, d))],
        out_specs=[full((n, d)), full((n, dff))],
        out_shape=[jax.ShapeDtypeStruct((n, d), F32), jax.ShapeDtypeStruct((n, dff), F32)],
        compiler_params=_cparams(("arbitrary",), VMEM_LIMIT),
        name="conv_ffn_rows",
    )(x, shift, scale, gate, h0, h1, wu, wg, cw, cb.reshape(1, dff), wd, g.reshape(1, d), b.reshape(1, d))


def _idx_kernel(pt_ref, qi_ref, wi_ref, *rest, pages):
    page_refs, out_ref = rest[:pages], rest[pages]
    qi = qi_ref[...]
    wi = wi_ref[:, 0:1]
    for k in range(pages):
        logits = _dot3(qi, page_refs[k][...], _NT) * (D_IDX ** -0.5)
        out_ref[k] = jnp.sum(jnp.maximum(logits, 0.0) * wi, axis=0, keepdims=True)


def _idx_call(page_table_flat, qi8, wi8, cache_idx, layer, n_pages, pages):
    bs = qi8.shape[0]
    page = cache_idx.shape[2]

    def page_spec(k):
        return pl.BlockSpec((None, None, page, D_IDX),
                            lambda s, j, pt: (layer, pt[s * n_pages + j * pages + k], 0, 0))

    return pl.pallas_call(
        functools.partial(_idx_kernel, pages=pages),
        grid_spec=pltpu.PrefetchScalarGridSpec(
            num_scalar_prefetch=1,
            grid=(bs, n_pages // pages),
            in_specs=[pl.BlockSpec((None, 8, D_IDX), lambda s, j, pt: (s, 0, 0)),
                      pl.BlockSpec((None, 8, LANE), lambda s, j, pt: (s, 0, 0))]
                     + [page_spec(k) for k in range(pages)],
            out_specs=pl.BlockSpec((pages, None, 1, page), lambda s, j, pt: (j, s, 0, 0))),
        out_shape=jax.ShapeDtypeStruct((n_pages, bs, 1, page), F32),
        compiler_params=_cparams(("arbitrary", "arbitrary")),
        name="decode_index_scores",
    )(page_table_flat, qi8, wi8, *([cache_idx] * pages))


def _thr_kernel(score_ref, q_ref, tail_ref, bias_ref, bias_self_ref, key_ref, *, kk):
    n_pages, bs, page = score_ref.shape
    tail = tail_ref[...]
    ki = tail[:, 0:D_IDX]
    s_self = jnp.zeros((bs, 1), F32)
    for h in range(H_IDX):
        logit = jnp.sum(q_ref[:, Q_QI + h * D_IDX:Q_QI + (h + 1) * D_IDX] * ki, axis=-1, keepdims=True)
        s_self = s_self + jnp.maximum(logit * (D_IDX ** -0.5), 0.0) * tail[:, D_IDX + h:D_IDX + h + 1]
    key_self = _sort_key(s_self)
    key_ref[...] = _sort_key(score_ref[...])

    def count(pred):
        per_lane = jnp.sum(pred(key_ref[...]).astype(jnp.int32), axis=0)
        return jnp.sum(per_lane, axis=-1, keepdims=True) + pred(key_self).astype(jnp.int32)

    thr = _kth_largest_key(lambda cand: count(lambda k: k >= cand), kk, bs)
    need = (kk - count(lambda k: k > thr)).astype(F32)
    tri = (lax.broadcasted_iota(jnp.int32, (page, page), 0)
           <= lax.broadcasted_iota(jnp.int32, (page, page), 1)).astype(BF16)

    def body(j, seen):
        key = key_ref[j]
        eq = key == thr
        cum = seen + _dot(eq.astype(BF16), tri)
        sel = (key > thr) | (eq & (cum <= need))
        bias_ref[j] = jnp.where(sel, 0.0, NEG_INF).astype(F32)
        return seen + jnp.sum(eq.astype(F32), axis=-1, keepdims=True)

    seen = lax.fori_loop(0, n_pages, body, jnp.zeros((bs, 1), F32))
    eq_self = key_self == thr
    sel_self = (key_self > thr) | (eq_self & (seen + 1.0 <= need))
    bias_self_ref[...] = jnp.broadcast_to(jnp.where(sel_self, 0.0, NEG_INF).astype(F32), (bs, LANE))


def _thr_call(scores, q_s, tail_s, kk):
    n_pages, bs, page = scores.shape
    full = lambda a: pl.BlockSpec(a.shape, lambda i: (0,) * a.ndim)
    return pl.pallas_call(
        functools.partial(_thr_kernel, kk=kk),
        grid=(1,),
        in_specs=[full(scores), full(q_s), full(tail_s)],
        out_specs=[pl.BlockSpec((n_pages, bs, page), lambda i: (0, 0, 0)),
                   pl.BlockSpec((bs, LANE), lambda i: (0, 0))],
        out_shape=[jax.ShapeDtypeStruct((n_pages, bs, page), F32), jax.ShapeDtypeStruct((bs, LANE), F32)],
        scratch_shapes=[pltpu.VMEM((n_pages, bs, page), jnp.int32)],
        compiler_params=_cparams(("arbitrary",), VMEM_LIMIT),
        name="decode_topk_bias",
    )(scores, q_s, tail_s)


def _dec_kernel(pt_ref, q_ref, kvn_ref, bias_ref, bself_ref, *rest, pages):
    page_refs = rest[:pages]
    fin_ref, pm_ref, pl_ref, pa_ref, ks_ref, m_ref, l_ref, acc_ref = rest[pages:]
    j = pl.program_id(1)
    nj = pl.num_programs(1)
    rows = DEC_ROWS
    row = lax.broadcasted_iota(jnp.int32, (rows, 1), 0)
    is_dsa = row >= 24
    is_moba = (row >= 16) & (row < 24)
    scale = jnp.where(row < 16, DQK_A ** -0.5, HEAD_DIM ** -0.5).astype(F32)
    qf = q_ref[...]
    qb = qf.astype(BF16)

    @pl.when(j == 0)
    def _():
        kvn = kvn_ref[...]
        s_self = jnp.sum(qf * kvn, axis=-1, keepdims=True) * scale
        s_self = s_self + jnp.where(is_dsa, bself_ref[:, 0:1], 0.0)
        m0 = jnp.maximum(s_self, M_INIT)
        p0 = jnp.exp(s_self - m0)
        m_ref[...] = jnp.broadcast_to(m0, m_ref.shape)
        l_ref[...] = jnp.broadcast_to(p0, l_ref.shape)
        acc_ref[...] = p0 * kvn

    m_ref[16:24, :] = jnp.full((8, LANE), M_INIT, F32)
    l_ref[16:24, :] = jnp.zeros((8, LANE), F32)
    acc_ref[16:24, :] = jnp.zeros((8, KVW), F32)

    ksum = jnp.zeros((1, H_B * ROW), F32)
    for k in range(pages):
        kp = page_refs[k][...]
        ksum = ksum + jnp.sum(kp[:, H_A * ROW:(H_A + H_B) * ROW], axis=0, keepdims=True)
        kpb = kp.astype(BF16)
        s = _dot(qb, kpb, _NT) * scale + jnp.where(is_dsa, bias_ref[k], 0.0)
        carry = (m_ref[:, 0:1], l_ref[:, 0:1], acc_ref[...])
        m, l, acc = _flash_step(carry, s, kpb)
        m_ref[...] = jnp.broadcast_to(m, m_ref.shape)
        l_ref[...] = jnp.broadcast_to(l, l_ref.shape)
        acc_ref[...] = acc

    pm_ref[pl.ds(j, 1)] = m_ref[16:24, :].reshape(1, 8, LANE)
    pl_ref[pl.ds(j, 1)] = l_ref[16:24, :].reshape(1, 8, LANE)
    row8 = lax.broadcasted_iota(jnp.int32, (8, 1), 0)
    pa_tile = jnp.zeros((8, ROW), F32)
    ks_tile = jnp.zeros((8, ROW), F32)
    for h in range(H_B):
        lo = (H_A + h) * ROW
        pa_tile = jnp.where(row8 == h, acc_ref[16:24, lo:lo + ROW], pa_tile)
        ks_tile = jnp.where(row8 == h, ksum[:, h * ROW:(h + 1) * ROW], ks_tile)
    pa_ref[pl.ds(j, 1)] = pa_tile.reshape(1, 8, ROW)
    ks_ref[pl.ds(j, 1)] = ks_tile.reshape(1, 8, ROW)

    @pl.when(j == nj - 1)
    def _():
        for h in range(H_A):
            for c in range(2):
                r = 2 * h + c
                v_lo = h * ROW + HEAD_DIM
                o = acc_ref[r:r + 1, v_lo:v_lo + HEAD_DIM] / l_ref[r:r + 1, 0:1]
                fin_ref[:, (c * H_A + h) * HEAD_DIM:(c * H_A + h + 1) * HEAD_DIM] = o
        for h in range(H_C):
            r = 24 + h
            v_lo = (H_A + H_B + h) * ROW + HEAD_DIM
            o = acc_ref[r:r + 1, v_lo:v_lo + HEAD_DIM] / l_ref[r:r + 1, 0:1]
            fin_ref[:, (2 * H_A + h) * HEAD_DIM:(2 * H_A + h + 1) * HEAD_DIM] = o


def _dec_call(page_table_flat, q_dec, kvn, bias, bias_self, cache_rows, layer, n_pages, pages):
    bs = q_dec.shape[0]
    page = cache_rows.shape[2]
    nblk = n_pages // pages
    fin_w = (2 * H_A + H_C) * HEAD_DIM

    def page_spec(k):
        return pl.BlockSpec((None, None, page, KVW),
                            lambda s, j, pt: (layer, pt[s * n_pages + j * pages + k], 0, 0))

    per_seq = lambda shape: pl.BlockSpec((None,) + shape, lambda s, j, pt: (s,) + (0,) * len(shape))
    return pl.pallas_call(
        functools.partial(_dec_kernel, pages=pages),
        grid_spec=pltpu.PrefetchScalarGridSpec(
            num_scalar_prefetch=1,
            grid=(bs, nblk),
            in_specs=[per_seq((DEC_ROWS, KVW)), per_seq((1, KVW)),
                      pl.BlockSpec((pages, None, 1, page), lambda s, j, pt: (j, s, 0, 0)),
                      per_seq((1, LANE))] + [page_spec(k) for k in range(pages)],
            out_specs=[per_seq((1, fin_w)), per_seq((nblk, 8, LANE)), per_seq((nblk, 8, LANE)),
                       per_seq((nblk, 8, ROW)), per_seq((nblk, 8, ROW))],
            scratch_shapes=[pltpu.VMEM((DEC_ROWS, LANE), F32), pltpu.VMEM((DEC_ROWS, LANE), F32),
                            pltpu.VMEM((DEC_ROWS, KVW), F32)]),
        out_shape=[jax.ShapeDtypeStruct((bs, 1, fin_w), F32),
                   jax.ShapeDtypeStruct((bs, nblk, 8, LANE), F32),
                   jax.ShapeDtypeStruct((bs, nblk, 8, LANE), F32),
                   jax.ShapeDtypeStruct((bs, nblk, 8, ROW), F32),
                   jax.ShapeDtypeStruct((bs, nblk, 8, ROW), F32)],
        compiler_params=_cparams(("arbitrary", "arbitrary"), VMEM_LIMIT),
        name="decode_paged_attention",
    )(page_table_flat, q_dec, kvn, bias, bias_self, *([cache_rows] * pages))


def _fin_kernel(lam_ref, fin_ref, pm_ref, pl_ref, pa_ref, ks_ref, q_ref, kvn_ref, g_ref,
                oa_ref, ob_ref, oc_ref, *, nblk, norm_scale):
    lam = lam_ref[0]
    fin = fin_ref[...]
    for h in range(H_A):
        o = (fin[:, h * HEAD_DIM:(h + 1) * HEAD_DIM]
             - lam * fin[:, (H_A + h) * HEAD_DIM:(H_A + h + 1) * HEAD_DIM])
        o = o * lax.rsqrt(jnp.mean(o * o, axis=-1, keepdims=True) + LN_EPS)
        oa_ref[:, h * HEAD_DIM:(h + 1) * HEAD_DIM] = (o * g_ref[...] * norm_scale).astype(oa_ref.dtype)
    oc_ref[...] = fin[:, 2 * H_A * HEAD_DIM:].astype(oc_ref.dtype)

    blk = lax.broadcasted_iota(jnp.int32, (nblk, 1), 0)
    ksel = min(MOBA_TOPK, nblk)
    for h in range(H_B):
        qh = q_ref[:, Q_QB + h * HEAD_DIM:Q_QB + (h + 1) * HEAD_DIM]
        mean = ks_ref[:, h, 0:HEAD_DIM] * (1.0 / MOBA_BLOCK)
        gate = jnp.sum(mean * qh, axis=-1, keepdims=True)
        sel = jnp.zeros((nblk, 1), jnp.bool_)
        for _ in range(ksel):
            best = jnp.max(gate, axis=0, keepdims=True)
            first = jnp.min(jnp.where(gate == best, blk, nblk), axis=0, keepdims=True)
            pick = blk == first
            sel = sel | pick
            gate = jnp.where(pick, -jnp.inf, gate)
        k_lo = (H_A + h) * ROW
        k_self = kvn_ref[:, k_lo:k_lo + HEAD_DIM]
        v_self = kvn_ref[:, k_lo + HEAD_DIM:k_lo + ROW]
        s_self = jnp.sum(qh * k_self, axis=-1, keepdims=True) * (HEAD_DIM ** -0.5)
        m_b = pm_ref[:, h, 0:1]
        l_b = pl_ref[:, h, 0:1]
        m_star = jnp.maximum(jnp.max(jnp.where(sel, m_b, M_INIT), axis=0, keepdims=True), s_self)
        w_b = jnp.where(sel, jnp.exp(m_b - m_star), 0.0)
        w_self = jnp.exp(s_self - m_star)
        l_star = jnp.sum(w_b * l_b, axis=0, keepdims=True) + w_self
        acc = jnp.sum(w_b * pa_ref[:, h, HEAD_DIM:ROW], axis=0, keepdims=True) + w_self * v_self
        ob_ref[:, h * HEAD_DIM:(h + 1) * HEAD_DIM] = (acc / l_star).astype(ob_ref.dtype)


def _fin_call(lam, fin, pm, pl_, pa, ks, q_s, kvn, g, norm_scale):
    bs, nblk = pm.shape[0], pm.shape[1]
    per_seq = lambda a: pl.BlockSpec((None,) + a.shape[1:], lambda s: (s,) + (0,) * (a.ndim - 1))
    q3 = q_s.reshape(bs, 1, QW)
    out = lambda w: pl.BlockSpec((None, 1, w), lambda s: (s, 0, 0))
    return pl.pallas_call(
        functools.partial(_fin_kernel, nblk=nblk, norm_scale=norm_scale),
        grid=(bs,),
        in_specs=[pl.BlockSpec(memory_space=pltpu.SMEM), per_seq(fin), per_seq(pm), per_seq(pl_), per_seq(pa),
                  per_seq(ks), per_seq(q3), per_seq(kvn), pl.BlockSpec((1, HEAD_DIM), lambda s: (0, 0))],
        out_specs=[out(H_A * HEAD_DIM), out(H_B * HEAD_DIM), out(H_C * HEAD_DIM)],
        out_shape=[jax.ShapeDtypeStruct((bs, 1, H_A * HEAD_DIM), BF16),
                   jax.ShapeDtypeStruct((bs, 1, H_B * HEAD_DIM), BF16),
                   jax.ShapeDtypeStruct((bs, 1, H_C * HEAD_DIM), BF16)],
        compiler_params=_cparams(("arbitrary",)),
        name="decode_finish",
    )(lam, fin, pm, pl_, pa, ks, q3, kvn, g)


def kernel(x_prompt, x_sample, cache_kv, cache_index_k, page_table, state_ffn_conv, c_prompt, c_sample, w_in, w_out, diff_lambda, diff_norm_g, w_ada_attn, b_ada_attn, ln1_g, ln1_b, w_ada_ffn, b_ada_ffn, w_up, w_gate, conv_w, conv_b, w_down, ln2_g, ln2_b):
    b, t, d = x_prompt.shape
    bs, ts, _ = x_sample.shape
    depth, n_pool, page, n_heads, row_w = cache_kv.shape
    n_pages = page_table.shape[1]
    past = n_pages * page
    dff = w_up.shape[2]
    assert ts == 1 and n_heads == N_HEADS and row_w == ROW and d == N_HEADS * HEAD_DIM
    assert conv_w.shape[1] == 3 and MOBA_BLOCK % page == 0 and past % MOBA_BLOCK == 0
    assert t % MOBA_BLOCK == 0 and bs % 8 == 0
    alpha = (2 * depth) ** 0.25
    pages_per_block = MOBA_BLOCK // page
    tq = MOBA_BLOCK
    tm_proj = min(512, t)
    tm_ffn = min(256, t)
    topk_prompt = min(DSA_TOPK, t // 4)
    n_keys_s = past + ts
    kk_sample = min(min(DSA_TOPK, n_keys_s // 4), n_keys_s)
    idx_pages = math.gcd(n_pages, 8)

    c_all = jnp.concatenate([c_prompt, c_sample], axis=0)
    m_attn = _ada_call(c_all, w_ada_attn, b_ada_attn)
    m_ffn = _ada_call(c_all, w_ada_ffn, b_ada_ffn)

    def mods(m, l):
        parts = []
        for k in range(3):
            mk = m[l, :, k * d:(k + 1) * d]
            parts.append((mk[:b].reshape(b, 1, d), mk[b:].reshape(1, bs, d)))
        return parts

    tabs_p = _rope_tables(jnp.arange(t, dtype=jnp.int32))
    tabs_s = _rope_tables(jnp.full((bs,), past, jnp.int32))
    pt_flat = page_table.reshape(-1).astype(jnp.int32)
    cache_rows = cache_kv.reshape(depth, n_pool, page, KVW)
    perm = jnp.asarray(_PERM)
    dec_qidx = jnp.asarray(_DEC_QIDX)

    xp = x_prompt
    xs = x_sample.reshape(1, bs, d)
    kvp, kvs, ikp, iks, cvp, cvs = [], [], [], [], [], []
    for l in range(depth):
        dl = diff_lambda[l].astype(F32)
        lam_init = 0.8 - 0.6 * math.exp(-0.3 * l)
        lam = (jnp.exp(jnp.sum(dl[0] * dl[1])) - jnp.exp(jnp.sum(dl[2] * dl[3])) + lam_init).reshape(1)
        norm_scale = 1.0 - lam_init
        wp = jnp.concatenate([w_in[l], jnp.zeros((d, 1), F32)], axis=1)[:, perm].astype(BF16)
        wo = w_out[l].astype(BF16)
        wu, wg, wd = w_up[l].astype(BF16), w_gate[l].astype(BF16), w_down[l].astype(BF16)
        gnorm = diff_norm_g[l].reshape(1, HEAD_DIM)
        (shift_p, shift_s), (scale_p, scale_s), (gate_p, gate_s) = mods(m_attn, l)
        (fshift_p, fshift_s), (fscale_p, fscale_s), (fgate_p, fgate_s) = mods(m_ffn, l)

        q, kv, kva, kvb, kvc, tail = _proj_call(xp, shift_p, scale_p, wp, tabs_p, tm_proj)
        oa = _diff_call(lam, q, kva, gnorm, tq, norm_scale)
        ob = _moba_call(q, kvb)
        oc = _dsa_call(q, tail, kvc, tq, topk_prompt)
        x1 = _out_call(oa, ob, oc, xp, gate_p, wo, ln1_g[l], ln1_b[l], tm_proj, alpha)
        xp, conv_p = _ffn_seq_call(x1, fshift_p, fscale_p, fgate_p, wu, wg, conv_w[l], conv_b[l], wd,
                                   ln2_g[l], ln2_b[l], tm_ffn, alpha)
        kvp.append(kv.reshape(b, t, N_HEADS, ROW))
        ikp.append(tail[:, :, :D_IDX])
        cvp.append(conv_p)

        q_s, kv_s, _, _, _, tail_s = _proj_call(xs, shift_s, scale_s, wp, tabs_s, bs)
        q_s, kv_s, tail_s = q_s[0], kv_s[0], tail_s[0]
        qi8 = jnp.pad(q_s[:, Q_QI:Q_QI + H_IDX * D_IDX].reshape(bs, H_IDX, D_IDX), ((0, 0), (0, 4), (0, 0)))
        wi8 = jnp.broadcast_to(jnp.pad(tail_s[:, D_IDX:D_IDX + H_IDX], ((0, 0), (0, 4)))[:, :, None],
                               (bs, 8, LANE))
        scores = _idx_call(pt_flat, qi8, wi8, cache_index_k, l, n_pages, idx_pages)
        bias, bias_self = _thr_call(scores.reshape(n_pages, bs, page), q_s, tail_s, kk_sample)
        q_dec = jnp.concatenate([q_s, jnp.zeros((bs, 1), F32)], axis=1)[:, dec_qidx].reshape(bs, DEC_ROWS, KVW)
        kvn = kv_s.reshape(bs, 1, KVW)
        fin, pm, pl_, pa, ks = _dec_call(pt_flat, q_dec, kvn, bias.reshape(n_pages, bs, 1, page),
                                         bias_self.reshape(bs, 1, LANE), cache_rows, l, n_pages,
                                         pages_per_block)
        oa_s, ob_s, oc_s = _fin_call(lam, fin, pm, pl_, pa, ks, q_s, kvn, gnorm, norm_scale)
        x1s = _out_call(oa_s.reshape(1, bs, -1), ob_s.reshape(1, bs, -1), oc_s.reshape(1, bs, -1), xs, gate_s,
                        wo, ln1_g[l], ln1_b[l], bs, alpha)
        hist = state_ffn_conv[l]
        x2s, g_s = _ffn_rows_call(x1s[0], fshift_s[0], fscale_s[0], fgate_s[0], hist[:, 0], hist[:, 1],
                                  wu, wg, conv_w[l], conv_b[l], wd, ln2_g[l], ln2_b[l], alpha)
        xs = x2s.reshape(1, bs, d)
        kvs.append(kv_s.reshape(bs, ts, N_HEADS, ROW))
        iks.append(tail_s[:, :D_IDX].reshape(bs, ts, D_IDX))
        cvs.append(jnp.stack([hist[:, 1], g_s], axis=1))

    return (xp, xs.reshape(bs, ts, d), jnp.stack(kvp), jnp.stack(kvs), jnp.stack(ikp), jnp.stack(iks),
            jnp.stack(cvp), jnp.stack(cvs))
```

```python
import functools
import math

import jax
import jax.numpy as jnp
import numpy as np
from jax import lax
from jax.experimental import pallas as pl
from jax.experimental.pallas import tpu as pltpu

F32 = jnp.float32
BF16 = jnp.bfloat16

HEAD_DIM = 64
N_HEADS = 16
H_B = 4
H_A = 6
H_C = 6
DQK_A = 32
ROW = 2 * HEAD_DIM
H_IDX = 4
D_IDX = 64
ROPE_THETA = 500000.0
MOBA_BLOCK = 256
MOBA_TOPK = 3
DSA_TOPK = 256
Q_BLOCK = 128
LN_EPS = 1e-5
NEG_INF = -1e30
M_INIT = -1e20
INT_MIN = -2 ** 31

OFF_QA, OFF_KA, OFF_VA = 0, 384, 768
OFF_QB, OFF_KB, OFF_VB = 1152, 1408, 1664
OFF_QC, OFF_KC, OFF_VC = 1920, 2304, 2688
OFF_QI, OFF_KI, OFF_WI = 3072, 3328, 3392
D_IN = 3396

LANE = 128
SUBLANE = 8
QW = 1280
Q_QA, Q_QC, Q_QB, Q_QI = 0, 384, 768, 1024
KVW = N_HEADS * ROW
TAILW = 128
WP = QW + KVW + TAILW
VMEM_LIMIT = 56 * 1024 * 1024

PAT_QA, PAT_QB, PAT_KA, PAT_KB, PAT_TAIL = 0, 1, 2, 3, 4
PAT_HALF = (4, 8, 4, 8, 8)


def _perm_segments():
    seg = [(OFF_QA, 384), (OFF_QC, 384), (OFF_QB, 256), (OFF_QI, 256)]
    for h in range(H_A):
        seg += [(OFF_KA + h * 64, 64), (OFF_VA + h * 64, 64)]
    for h in range(H_B):
        seg += [(OFF_KB + h * 64, 64), (OFF_VB + h * 64, 64)]
    for h in range(H_C):
        seg += [(OFF_KC + h * 64, 64), (OFF_VC + h * 64, 64)]
    seg += [(OFF_KI, 64), (OFF_WI, 4)]
    return seg


_PERM_SEGMENTS = _perm_segments()
_GROUP_PAT = ([PAT_QA] * 3 + [PAT_QB] * 7 + [PAT_KA] * H_A + [PAT_KB] * (H_B + H_C) + [PAT_TAIL])

DEC_ROWS = 32
DEC_HEAD_OF_ROW = np.zeros((DEC_ROWS,), np.int32)
for _h in range(H_A):
    DEC_HEAD_OF_ROW[2 * _h] = DEC_HEAD_OF_ROW[2 * _h + 1] = _h
for _h in range(H_B):
    DEC_HEAD_OF_ROW[16 + _h] = DEC_HEAD_OF_ROW[20 + _h] = H_A + _h
for _h in range(H_C):
    DEC_HEAD_OF_ROW[24 + _h] = H_A + H_B + _h


def _decode_q_index():
    idx = np.full((DEC_ROWS, ROW), QW, np.int32)
    for h in range(H_A):
        for c in range(2):
            idx[2 * h + c, c * 32:c * 32 + 32] = np.arange(32) + Q_QA + h * 64 + c * 32
    for h in range(H_B):
        idx[16 + h, 0:64] = np.arange(64) + Q_QB + h * 64
    for h in range(H_C):
        idx[24 + h, 0:64] = np.arange(64) + Q_QC + h * 64
    return idx.reshape(-1)


_DEC_QIDX = _decode_q_index()


def _split(x):
    hi = x.astype(BF16)
    lo = (x - hi.astype(F32)).astype(BF16)
    return hi, lo


_NN = (((1,), (0,)), ((), ()))
_NT = (((1,), (1,)), ((), ()))


def _dot(a, b, dims=_NN):
    return lax.dot_general(a, b, dims, preferred_element_type=F32)


def _dot3(a, b, dims=_NN):
    ah, al = _split(a)
    bh, bl = _split(b)
    return _dot(ah, bh, dims) + (_dot(ah, bl, dims) + _dot(al, bh, dims))


def _layer_norm(x, g, b):
    mu = jnp.mean(x, axis=-1, keepdims=True)
    xc = x - mu
    var = jnp.mean(xc * xc, axis=-1, keepdims=True)
    return xc * lax.rsqrt(var + LN_EPS) * g + b


def _cparams(sem, vmem=None):
    return pltpu.CompilerParams(dimension_semantics=sem, vmem_limit_bytes=vmem)


def _ada_kernel(c_ref, w_ref, b_ref, o_ref):
    c = c_ref[...]
    a = c / (1.0 + jnp.exp(-c))
    o_ref[...] = _dot3(a, w_ref[...]) + b_ref[...]


def _ada_call(c_all, w, b):
    depth, d, d3 = w.shape
    n = c_all.shape[0]
    return pl.pallas_call(
        _ada_kernel,
        grid=(depth, d3 // d),
        in_specs=[pl.BlockSpec((n, d), lambda l, j: (0, 0)),
                  pl.BlockSpec((None, d, d), lambda l, j: (l, 0, j)),
                  pl.BlockSpec((None, 1, d), lambda l, j: (l, 0, j))],
        out_specs=pl.BlockSpec((None, n, d), lambda l, j: (l, 0, j)),
        out_shape=jax.ShapeDtypeStruct((depth, n, d3), F32),
        compiler_params=_cparams(("arbitrary", "arbitrary")),
        name="ada_modulation",
    )(c_all, w, b.reshape(depth, 1, d3))


def _rope_tables(pos):
    posf = pos.astype(F32)[:, None]
    r = pos.shape[0]

    def family(half, group):
        inv = jnp.power(ROPE_THETA, -jnp.arange(half, dtype=F32) / half)
        ang = posf * inv[None, :]
        cos, sin = jnp.cos(ang), jnp.sin(ang)
        rest = group - 2 * half
        cg = jnp.concatenate([cos, cos, jnp.ones((r, rest), F32)], axis=-1)
        sg = jnp.concatenate([-sin, sin, jnp.zeros((r, rest), F32)], axis=-1)
        return cg, sg

    ca, sa = family(4, 32)
    cb, sb = family(8, 64)
    one64, zero64 = jnp.ones((r, 64), F32), jnp.zeros((r, 64), F32)
    tail_c = jnp.concatenate([jnp.full((r, 4), H_IDX ** -0.5, F32), jnp.ones((r, 60), F32)], axis=-1)
    tabs = [jnp.tile(ca, (1, 4)), jnp.tile(sa, (1, 4)),
            jnp.tile(cb, (1, 2)), jnp.tile(sb, (1, 2)),
            jnp.concatenate([ca, ca, one64], -1), jnp.concatenate([sa, sa, zero64], -1),
            jnp.concatenate([cb, one64], -1), jnp.concatenate([sb, zero64], -1),
            jnp.concatenate([cb, tail_c], -1), jnp.concatenate([sb, zero64], -1)]
    return jnp.stack(tabs)


def _proj_kernel(x_ref, shift_ref, scale_ref, w_ref, tab_ref,
                 q_ref, kv_ref, kva_ref, kvb_ref, kvc_ref, tail_ref):
    h = x_ref[...] * (1.0 + scale_ref[...]) + shift_ref[...]
    hb = h.astype(BF16)
    lane = lax.broadcasted_iota(jnp.int32, (1, LANE), 1)
    x1_mask = {4: (lane % 32) < 4, 8: (lane % 64) < 8}

    def rope(y, g):
        pat = _GROUP_PAT[g]
        half = PAT_HALF[pat]
        partner = jnp.where(x1_mask[half], pltpu.roll(y, LANE - half, 1), pltpu.roll(y, half, 1))
        return y * tab_ref[2 * pat] + partner * tab_ref[2 * pat + 1]

    n_groups = WP // LANE
    g = 0
    while g < n_groups:
        width = min(2, n_groups - g)
        y = _dot(hb, w_ref[:, g * LANE:(g + width) * LANE])
        for k in range(width):
            gg = g + k
            r = rope(y[:, k * LANE:(k + 1) * LANE], gg)
            c0 = gg * LANE
            if c0 < QW:
                q_ref[:, c0:c0 + LANE] = r
            elif c0 < QW + KVW:
                c = c0 - QW
                kv_ref[:, c:c + LANE] = r
                rb = r.astype(BF16)
                head = c // ROW
                if head < H_A:
                    kva_ref[:, c:c + LANE] = rb
                elif head < H_A + H_B:
                    kvb_ref[:, c - H_A * ROW:c - H_A * ROW + LANE] = rb
                else:
                    cc = c - (H_A + H_B) * ROW
                    kvc_ref[:, cc:cc + LANE] = rb
            else:
                tail_ref[...] = r
        g += width


def _proj_call(x, shift, scale, wp, tabs, tm):
    g_n, tg, d = x.shape
    rm = shift.shape[1]
    nt = tg // tm
    if rm == 1:
        mod_spec = pl.BlockSpec((None, 1, d), lambda i, g: (g, 0, 0))
    else:
        mod_spec = pl.BlockSpec((None, tm, d), lambda i, g: (g, i, 0))

    def row_spec(w):
        return pl.BlockSpec((None, tm, w), lambda i, g: (g, i, 0))

    widths = (QW, KVW, H_A * ROW, H_B * ROW, H_C * ROW, TAILW)
    dtypes = (F32, F32, BF16, BF16, BF16, F32)
    return pl.pallas_call(
        _proj_kernel,
        grid=(nt, g_n),
        in_specs=[row_spec(d), mod_spec, mod_spec,
                  pl.BlockSpec((d, WP), lambda i, g: (0, 0)),
                  pl.BlockSpec((10, tm, LANE), lambda i, g: (0, i, 0))],
        out_specs=[row_spec(w) for w in widths],
        out_shape=[jax.ShapeDtypeStruct((g_n, tg, w), dt) for w, dt in zip(widths, dtypes)],
        compiler_params=_cparams(("arbitrary", "arbitrary"), VMEM_LIMIT),
        name="in_proj_rope",
    )(x, shift, scale, wp, tabs)


def _chain_reset(m_ref, l_ref, acc_ref):
    m_ref[...] = jnp.full(m_ref.shape, M_INIT, F32)
    l_ref[...] = jnp.zeros(l_ref.shape, F32)
    acc_ref[...] = jnp.zeros(acc_ref.shape, F32)


def _chains_update(scores, values, s_ref, p_ref, a_ref, m_ref, l_ref, acc_ref):
    n_chains = len(scores)
    for n in range(n_chains):
        s_ref[n] = scores[n]()
    for n in range(n_chains):
        r = n * SUBLANE
        s = s_ref[n]
        m_old = m_ref[r:r + 1, :]
        m_new = jnp.maximum(m_old, jnp.max(s, axis=0, keepdims=True))
        alpha = jnp.exp(m_old - m_new)
        p = jnp.exp(s - m_new)
        l_ref[r:r + 1, :] = alpha * l_ref[r:r + 1, :] + jnp.sum(p, axis=0, keepdims=True)
        p_ref[n] = p.astype(BF16)
        a_ref[r:r + 1, :] = alpha
        m_ref[r:r + 1, :] = m_new
    for n in range(n_chains):
        vt = values[n]()
        dv = vt.shape[0]
        acc_ref[n * dv:(n + 1) * dv, :] = (a_ref[n * SUBLANE:n * SUBLANE + 1, :] * acc_ref[n * dv:(n + 1) * dv, :]
                                           + _dot(vt, p_ref[n]))


def _chain_scratch(chains, tq, dv):
    return [pltpu.VMEM((chains, tq, tq), F32), pltpu.VMEM((chains, tq, tq), BF16),
            pltpu.VMEM((chains * SUBLANE, tq), F32), pltpu.VMEM((chains * SUBLANE, tq), F32),
            pltpu.VMEM((chains * SUBLANE, tq), F32), pltpu.VMEM((chains * dv, tq), F32)]


def _chain_result(n, dv, l_ref, acc_ref):
    return acc_ref[n * dv:(n + 1) * dv, :] / l_ref[n * SUBLANE:n * SUBLANE + 1, :]


def _causal_bias_t(size):
    kpos = lax.broadcasted_iota(jnp.int32, (size, size), 0)
    qpos = lax.broadcasted_iota(jnp.int32, (size, size), 1)
    return jnp.where(kpos <= qpos, 0.0, NEG_INF).astype(F32)


def _build_vt(kv_ref, vt_ref, n_heads, n_chunks, ck):
    for c in range(n_chunks):
        for h in range(n_heads):
            x = kv_ref[c * ck:(c + 1) * ck, h * ROW:(h + 1) * ROW].astype(F32)
            vt_ref[c, h * HEAD_DIM:(h + 1) * HEAD_DIM, :] = x.T[HEAD_DIM:ROW, :].astype(BF16)


def _diff_kernel(lam_ref, q_ref, kv_ref, g_ref, o_ref, vt_ref, qs_ref, s_ref, p_ref, a_ref, m_ref, l_ref, acc_ref,
                 *, tq, n_chunks, norm_scale):
    i = pl.program_id(1)
    lam = lam_ref[0]

    @pl.when(i == 0)
    def _():
        _build_vt(kv_ref, vt_ref, H_A, n_chunks, tq)

    for n in range(2 * H_A):
        qs_ref[n] = (q_ref[:, n * DQK_A:(n + 1) * DQK_A] * (DQK_A ** -0.5)).astype(BF16)
    _chain_reset(m_ref, l_ref, acc_ref)

    def update(j, extra):
        r0 = pl.multiple_of(j * tq, tq)

        def score(n):
            h, c = divmod(n, 2)
            k = kv_ref[pl.ds(r0, tq), h * ROW + c * DQK_A:h * ROW + (c + 1) * DQK_A]
            s = _dot(k, qs_ref[n], _NT)
            return s if extra is None else s + extra

        def value(n):
            h = n // 2
            return vt_ref[j, h * HEAD_DIM:(h + 1) * HEAD_DIM, :]

        _chains_update([functools.partial(score, n) for n in range(2 * H_A)],
                       [functools.partial(value, n) for n in range(2 * H_A)],
                       s_ref, p_ref, a_ref, m_ref, l_ref, acc_ref)

    def body(j, carry):
        update(j, None)
        return carry

    lax.fori_loop(0, i, body, 0)
    update(i, _causal_bias_t(tq))

    outs = []
    for h in range(H_A):
        o = (_chain_result(2 * h, HEAD_DIM, l_ref, acc_ref)
             - lam * _chain_result(2 * h + 1, HEAD_DIM, l_ref, acc_ref))
        o = o * lax.rsqrt(jnp.mean(o * o, axis=0, keepdims=True) + LN_EPS)
        outs.append(o * g_ref[...] * norm_scale)
    o_ref[...] = jnp.concatenate(outs, axis=0).T.astype(o_ref.dtype)


def _diff_call(lam, q, kva, g_col, tq, norm_scale):
    b, t, _ = q.shape
    n_chunks = t // tq
    chains = 2 * H_A
    return pl.pallas_call(
        functools.partial(_diff_kernel, tq=tq, n_chunks=n_chunks, norm_scale=norm_scale),
        grid=(b, n_chunks),
        in_specs=[pl.BlockSpec(memory_space=pltpu.SMEM),
                  pl.BlockSpec((None, tq, 384), lambda b_, i: (b_, i, Q_QA // 384)),
                  pl.BlockSpec((None, t, H_A * ROW), lambda b_, i: (b_, 0, 0)),
                  pl.BlockSpec((HEAD_DIM, 1), lambda b_, i: (0, 0))],
        out_specs=pl.BlockSpec((None, tq, H_A * HEAD_DIM), lambda b_, i: (b_, i, 0)),
        out_shape=jax.ShapeDtypeStruct((b, t, H_A * HEAD_DIM), BF16),
        scratch_shapes=[pltpu.VMEM((n_chunks, H_A * HEAD_DIM, tq), BF16),
                        pltpu.VMEM((chains, tq, DQK_A), BF16)] + _chain_scratch(chains, tq, HEAD_DIM),
        compiler_params=_cparams(("arbitrary", "arbitrary"), VMEM_LIMIT),
        name="diff_attention",
    )(lam, q, kva, g_col)


def _moba_kernel(q_ref, kv_ref, o_ref, mean_ref, vt_ref, qs_ref, gate_ref, s_ref, p_ref, a_ref, m_ref, l_ref,
                 acc_ref, *, n_blocks, nbr):
    i = pl.program_id(1)
    tq = MOBA_BLOCK

    @pl.when(i == 0)
    def _():
        mean_ref[...] = jnp.zeros_like(mean_ref)
        for blk in range(n_blocks):
            kblk = kv_ref[blk * tq:(blk + 1) * tq, :].astype(F32)
            mean_ref[blk:blk + 1, :] = jnp.mean(kblk, axis=0, keepdims=True)
        _build_vt(kv_ref, vt_ref, H_B, n_blocks, tq)

    for h in range(H_B):
        qf = q_ref[:, h * HEAD_DIM:(h + 1) * HEAD_DIM]
        qs_ref[h] = (qf * (HEAD_DIM ** -0.5)).astype(BF16)
        gate_ref[h * nbr:(h + 1) * nbr, :] = _dot3(mean_ref[:, h * ROW:h * ROW + HEAD_DIM], qf, _NT)
    _chain_reset(m_ref, l_ref, acc_ref)
    ksel = jnp.minimum(MOBA_TOPK, i)
    blk_id = lax.broadcasted_iota(jnp.int32, (nbr, 1), 0)

    def update(c, own):
        r0 = pl.multiple_of(c * tq, tq)

        def score(h):
            k = kv_ref[pl.ds(r0, tq), h * ROW:h * ROW + HEAD_DIM]
            s = _dot(k, qs_ref[h], _NT)
            if own:
                return s + _causal_bias_t(tq)
            gate = gate_ref[h * nbr:(h + 1) * nbr, :]
            col = jnp.sum(jnp.where(blk_id == c, gate, 0.0), axis=0, keepdims=True)
            beats = ((gate > col) | ((gate == col) & (blk_id < c))) & (blk_id < i)
            rank = jnp.sum(beats.astype(jnp.int32), axis=0, keepdims=True)
            return s + jnp.where(rank < ksel, 0.0, NEG_INF).astype(F32)

        def value(h):
            return vt_ref[c, h * HEAD_DIM:(h + 1) * HEAD_DIM, :]

        _chains_update([functools.partial(score, h) for h in range(H_B)],
                       [functools.partial(value, h) for h in range(H_B)],
                       s_ref, p_ref, a_ref, m_ref, l_ref, acc_ref)

    def body(c, carry):
        update(c, False)
        return carry

    lax.fori_loop(0, i, body, 0)
    update(i, True)
    outs = [_chain_result(h, HEAD_DIM, l_ref, acc_ref) for h in range(H_B)]
    o_ref[...] = jnp.concatenate(outs, axis=0).T.astype(o_ref.dtype)


def _moba_call(q, kvb):
    b, t, _ = q.shape
    tq = MOBA_BLOCK
    n_blocks = t // tq
    nbr = -(-n_blocks // SUBLANE) * SUBLANE
    return pl.pallas_call(
        functools.partial(_moba_kernel, n_blocks=n_blocks, nbr=nbr),
        grid=(b, n_blocks),
        in_specs=[pl.BlockSpec((None, tq, 256), lambda b_, i: (b_, i, Q_QB // 256)),
                  pl.BlockSpec((None, t, H_B * ROW), lambda b_, i: (b_, 0, 0))],
        out_specs=pl.BlockSpec((None, tq, H_B * HEAD_DIM), lambda b_, i: (b_, i, 0)),
        out_shape=jax.ShapeDtypeStruct((b, t, H_B * HEAD_DIM), BF16),
        scratch_shapes=[pltpu.VMEM((nbr, H_B * ROW), F32),
                        pltpu.VMEM((n_blocks, H_B * HEAD_DIM, tq), BF16),
                        pltpu.VMEM((H_B, tq, HEAD_DIM), BF16),
                        pltpu.VMEM((H_B * nbr, tq), F32)] + _chain_scratch(H_B, tq, HEAD_DIM),
        compiler_params=_cparams(("arbitrary", "arbitrary"), VMEM_LIMIT),
        name="moba_attention",
    )(q, kvb)


def _sort_key(score):
    bits = lax.bitcast_convert_type(jnp.where(score == 0.0, 0.0, score), jnp.int32)
    return jnp.where(bits < 0, bits ^ jnp.int32(0x7FFFFFFF), bits)


def _kth_largest_key(count_ge, kk, shape):
    def body(it, prefix):
        bit = lax.shift_left(jnp.int32(1), 31 - it)
        cand_u = prefix | bit
        cnt = count_ge(cand_u ^ jnp.int32(INT_MIN))
        return jnp.where(cnt >= kk, cand_u, prefix)

    prefix = lax.fori_loop(0, 32, body, jnp.zeros(shape, jnp.int32))
    return prefix ^ jnp.int32(INT_MIN)


def _dsa_kernel(q_ref, qi_ref, tq_ref, tk_ref, kv_ref, o_ref,
                key_ref, bias_ref, vt_ref, qs_ref, qih_ref, qil_ref, sel_ref, s_ref, p_ref, a_ref, m_ref, l_ref,
                acc_ref, *, tq, n_chunks_total, topk):
    i = pl.program_id(1)
    n_chunks = i + 1

    @pl.when(i == 0)
    def _():
        _build_vt(kv_ref, vt_ref, H_C, n_chunks_total, tq)

    for h in range(H_C):
        qs_ref[h] = (q_ref[:, h * HEAD_DIM:(h + 1) * HEAD_DIM] * (HEAD_DIM ** -0.5)).astype(BF16)
    for h in range(H_IDX):
        hi, lo = _split(qi_ref[:, h * D_IDX:(h + 1) * D_IDX])
        qih_ref[h] = hi
        qil_ref[h] = lo
    tail_t = tq_ref[...].T
    qpos = i * tq + lax.broadcasted_iota(jnp.int32, (1, tq), 1)
    kk = jnp.minimum(topk, ((qpos // Q_BLOCK) + 1) * Q_BLOCK)
    diag_bias = _causal_bias_t(tq)

    def score_chunk(j, masked):
        r0 = pl.multiple_of(j * tq, tq)
        ik_hi, ik_lo = _split(tk_ref[pl.ds(r0, tq), 0:D_IDX])
        score = jnp.zeros((tq, tq), F32)
        for h in range(H_IDX):
            logits = (_dot(ik_hi, qih_ref[h], _NT)
                      + (_dot(ik_hi, qil_ref[h], _NT) + _dot(ik_lo, qih_ref[h], _NT))) * (D_IDX ** -0.5)
            score = score + jnp.maximum(logits, 0.0) * tail_t[D_IDX + h:D_IDX + h + 1, :]
        if masked:
            score = jnp.where(diag_bias < 0.0, NEG_INF, score)
        key_ref[j] = _sort_key(score)

    def score_body(j, carry):
        score_chunk(j, False)
        return carry

    lax.fori_loop(0, i, score_body, 0)
    score_chunk(i, True)

    def count(pred):
        def body(j, cnt):
            hit = pred(key_ref[j]).astype(jnp.int32).reshape(tq // SUBLANE, SUBLANE, tq)
            return cnt + jnp.sum(hit, axis=0)
        cnt = lax.fori_loop(0, n_chunks, body, jnp.zeros((SUBLANE, tq), jnp.int32))
        return jnp.sum(cnt, axis=0, keepdims=True)

    @pl.when(n_chunks * tq <= topk)
    def _():
        sel_ref[0:1, :] = jnp.full((1, tq), INT_MIN, jnp.int32)
        sel_ref[SUBLANE:SUBLANE + 1, :] = jnp.full((1, tq), 2 ** 30, jnp.int32)

    @pl.when(n_chunks * tq > topk)
    def _():
        thr = _kth_largest_key(lambda cand: count(lambda k: k >= cand), kk, (1, tq))
        n_gt = count(lambda k: k > thr)
        sel_ref[0:1, :] = thr
        sel_ref[SUBLANE:SUBLANE + 1, :] = kk - n_gt

    thr = sel_ref[0:1, :]
    need = sel_ref[SUBLANE:SUBLANE + 1, :].astype(F32)
    tri = (lax.broadcasted_iota(jnp.int32, (tq, tq), 1)
           <= lax.broadcasted_iota(jnp.int32, (tq, tq), 0)).astype(BF16)

    def bias_body(j, seen):
        key = key_ref[j]
        eq = key == thr
        cum = seen + _dot(tri, eq.astype(BF16))
        sel = (key > thr) | (eq & (cum <= need))
        bias_ref[j] = jnp.where(sel, 0.0, NEG_INF).astype(F32)
        return seen + jnp.sum(eq.astype(F32), axis=0, keepdims=True)

    lax.fori_loop(0, n_chunks, bias_body, jnp.zeros((1, tq), F32))

    _chain_reset(m_ref, l_ref, acc_ref)

    def update(j, extra):
        r0 = pl.multiple_of(j * tq, tq)

        def score(h):
            k = kv_ref[pl.ds(r0, tq), h * ROW:h * ROW + HEAD_DIM]
            s = _dot(k, qs_ref[h], _NT) + bias_ref[j]
            return s if extra is None else s + extra

        def value(h):
            return vt_ref[j, h * HEAD_DIM:(h + 1) * HEAD_DIM, :]

        _chains_update([functools.partial(score, h) for h in range(H_C)],
                       [functools.partial(value, h) for h in range(H_C)],
                       s_ref, p_ref, a_ref, m_ref, l_ref, acc_ref)

    def body(j, carry):
        update(j, None)
        return carry

    lax.fori_loop(0, i, body, 0)
    update(i, diag_bias)
    outs = [_chain_result(h, HEAD_DIM, l_ref, acc_ref) for h in range(H_C)]
    o_ref[...] = jnp.concatenate(outs, axis=0).T.astype(o_ref.dtype)


def _dsa_call(q, tail, kvc, tq, topk):
    b, t, _ = q.shape
    n_chunks = t // tq
    return pl.pallas_call(
        functools.partial(_dsa_kernel, tq=tq, n_chunks_total=n_chunks, topk=topk),
        grid=(b, n_chunks),
        in_specs=[pl.BlockSpec((None, tq, 384), lambda b_, i: (b_, i, Q_QC // 384)),
                  pl.BlockSpec((None, tq, 256), lambda b_, i: (b_, i, Q_QI // 256)),
                  pl.BlockSpec((None, tq, TAILW), lambda b_, i: (b_, i, 0)),
                  pl.BlockSpec((None, t, TAILW), lambda b_, i: (b_, 0, 0)),
                  pl.BlockSpec((None, t, H_C * ROW), lambda b_, i: (b_, 0, 0))],
        out_specs=pl.BlockSpec((None, tq, H_C * HEAD_DIM), lambda b_, i: (b_, i, 0)),
        out_shape=jax.ShapeDtypeStruct((b, t, H_C * HEAD_DIM), BF16),
        scratch_shapes=[pltpu.VMEM((n_chunks, tq, tq), jnp.int32),
                        pltpu.VMEM((n_chunks, tq, tq), F32),
                        pltpu.VMEM((n_chunks, H_C * HEAD_DIM, tq), BF16),
                        pltpu.VMEM((H_C, tq, HEAD_DIM), BF16),
                        pltpu.VMEM((H_IDX, tq, D_IDX), BF16),
                        pltpu.VMEM((H_IDX, tq, D_IDX), BF16),
                        pltpu.VMEM((2 * SUBLANE, tq), jnp.int32)] + _chain_scratch(H_C, tq, HEAD_DIM),
        compiler_params=_cparams(("arbitrary", "arbitrary"), VMEM_LIMIT),
        name="dsa_attention",
    )(q, q, tail, tail, kvc)


def _out_kernel(oa_ref, ob_ref, oc_ref, x_ref, gate_ref, w_ref, g_ref, b_ref, y_ref, *, alpha):
    wa, wb = H_A * HEAD_DIM, (H_A + H_B) * HEAD_DIM
    o = (_dot(oa_ref[...], w_ref[0:wa, :]) + _dot(ob_ref[...], w_ref[wa:wb, :])
         + _dot(oc_ref[...], w_ref[wb:, :]))
    y = alpha * x_ref[...] + (1.0 + gate_ref[...]) * o
    y_ref[...] = _layer_norm(y, g_ref[...], b_ref[...])


def _out_call(oa, ob, oc, x, gate, w_out, g, b, tm, alpha):
    g_n, tg, d = x.shape
    rm = gate.shape[1]
    if rm == 1:
        mod_spec = pl.BlockSpec((None, 1, d), lambda g_, i: (g_, 0, 0))
    else:
        mod_spec = pl.BlockSpec((None, tm, d), lambda g_, i: (g_, i, 0))

    def row_spec(w):
        return pl.BlockSpec((None, tm, w), lambda g_, i: (g_, i, 0))

    vec = pl.BlockSpec((1, d), lambda g_, i: (0, 0))
    return pl.pallas_call(
        functools.partial(_out_kernel, alpha=alpha),
        grid=(g_n, tg // tm),
        in_specs=[row_spec(oa.shape[-1]), row_spec(ob.shape[-1]), row_spec(oc.shape[-1]), row_spec(d),
                  mod_spec, pl.BlockSpec((d, d), lambda g_, i: (0, 0)), vec, vec],
        out_specs=row_spec(d),
        out_shape=jax.ShapeDtypeStruct((g_n, tg, d), F32),
        compiler_params=_cparams(("arbitrary", "arbitrary"), VMEM_LIMIT),
        name="out_proj_ln",
    )(oa, ob, oc, x, gate, w_out, g.reshape(1, d), b.reshape(1, d))


def _ffn_tail(x, gate, u, gc, wd_ref, g_ref, b_ref, alpha):
    act = gc / (1.0 + jnp.exp(-gc)) * u
    f = _dot(act.astype(BF16), wd_ref[...])
    return _layer_norm(alpha * x + (1.0 + gate) * f, g_ref[...], b_ref[...])


def _ffn_seq_kernel(x_ref, shift_ref, scale_ref, gate_ref, wu_ref, wg_ref, cw_ref, cb_ref, wd_ref,
                    g_ref, b_ref, y_ref, conv_ref, carry_ref, *, alpha, tm):
    j = pl.program_id(1)

    @pl.when(j == 0)
    def _():
        carry_ref[...] = jnp.zeros_like(carry_ref)

    x = x_ref[...]
    hb = (x * (1.0 + scale_ref[...]) + shift_ref[...]).astype(BF16)
    u = _dot(hb, wu_ref[...])
    g = _dot(hb, wg_ref[...])
    row = lax.broadcasted_iota(jnp.int32, (tm, 1), 0)
    prev1, prev2 = carry_ref[1:2, :], carry_ref[0:1, :]
    g1 = jnp.where(row == 0, prev1, pltpu.roll(g, 1, 0))
    g2 = jnp.where(row == 0, prev2, jnp.where(row == 1, prev1, pltpu.roll(g, 2, 0)))
    gc = cb_ref[...] + g2 * cw_ref[0:1, :] + g1 * cw_ref[1:2, :] + g * cw_ref[2:3, :]
    last2 = g[tm - 2:tm, :]
    carry_ref[...] = last2
    conv_ref[...] = last2
    y_ref[...] = _ffn_tail(x, gate_ref[...], u, gc, wd_ref, g_ref, b_ref, alpha)


def _ffn_rows_kernel(x_ref, shift_ref, scale_ref, gate_ref, h0_ref, h1_ref, wu_ref, wg_ref, cw_ref, cb_ref,
                     wd_ref, g_ref, b_ref, y_ref, gout_ref, *, alpha):
    x = x_ref[...]
    hb = (x * (1.0 + scale_ref[...]) + shift_ref[...]).astype(BF16)
    u = _dot(hb, wu_ref[...])
    g = _dot(hb, wg_ref[...])
    gc = cb_ref[...] + h0_ref[...] * cw_ref[0:1, :] + h1_ref[...] * cw_ref[1:2, :] + g * cw_ref[2:3, :]
    gout_ref[...] = g
    y_ref[...] = _ffn_tail(x, gate_ref[...], u, gc, wd_ref, g_ref, b_ref, alpha)


def _ffn_seq_call(x, shift, scale, gate, wu, wg, cw, cb, wd, g, b, tm, alpha):
    g_n, tg, d = x.shape
    dff = wu.shape[1]
    mod_spec = pl.BlockSpec((None, 1, d), lambda g_, i: (g_, 0, 0))
    row_spec = pl.BlockSpec((None, tm, d), lambda g_, i: (g_, i, 0))
    const = lambda shape: pl.BlockSpec(shape, lambda g_, i: (0, 0))
    return pl.pallas_call(
        functools.partial(_ffn_seq_kernel, alpha=alpha, tm=tm),
        grid=(g_n, tg // tm),
        in_specs=[row_spec, mod_spec, mod_spec, mod_spec, const((d, dff)), const((d, dff)),
                  const((3, dff)), const((1, dff)), const((dff, d)), const((1, d)), const((1, d))],
        out_specs=[row_spec, pl.BlockSpec((None, 2, dff), lambda g_, i: (g_, 0, 0))],
        out_shape=[jax.ShapeDtypeStruct((g_n, tg, d), F32), jax.ShapeDtypeStruct((g_n, 2, dff), F32)],
        scratch_shapes=[pltpu.VMEM((2, dff), F32)],
        compiler_params=_cparams(("arbitrary", "arbitrary"), VMEM_LIMIT),
        name="conv_ffn_seq",
    )(x, shift, scale, gate, wu, wg, cw, cb.reshape(1, dff), wd, g.reshape(1, d), b.reshape(1, d))


def _ffn_rows_call(x, shift, scale, gate, h0, h1, wu, wg, cw, cb, wd, g, b, alpha):
    n, d = x.shape
    dff = wu.shape[1]
    full = lambda shape: pl.BlockSpec(shape, lambda i: (0, 0))
    return pl.pallas_call(
        functools.partial(_ffn_rows_kernel, alpha=alpha),
        grid=(1,),
        in_specs=[full((n, d)), full((n, d)), full((n, d)), full((n, d)), full((n, dff)), full((n, dff)),
                  full((d, dff)), full((d, dff)), full((3, dff)), full((1, dff)), full((dff, d)),
                  full((1, d)), full((1, d))],
        out_specs=[full((n, d)), full((n, dff))],
        out_shape=[jax.ShapeDtypeStruct((n, d), F32), jax.ShapeDtypeStruct((n, dff), F32)],
        compiler_params=_cparams(("arbitrary",), VMEM_LIMIT),
        name="conv_ffn_rows",
    )(x, shift, scale, gate, h0, h1, wu, wg, cw, cb.reshape(1, dff), wd, g.reshape(1, d), b.reshape(1, d))


def _idxsel_kernel(pt_ref, qi_ref, wi_ref, tail_ref, expand_ref, *rest, pages, kk):
    page_refs = rest[:pages]
    bias_ref, bias_self_ref, sc_ref = rest[pages:]
    j = pl.program_id(1)
    nj = pl.num_programs(1)
    n_pages, page = sc_ref.shape
    qi = qi_ref[...]
    wi = wi_ref[:, 0:1]
    rows = []
    for k in range(pages):
        logits = _dot3(qi, page_refs[k][...], _NT) * (D_IDX ** -0.5)
        rows.append(jnp.sum(jnp.maximum(logits, 0.0) * wi, axis=0, keepdims=True))
    sc_ref[pl.ds(pl.multiple_of(j * pages, pages), pages), :] = jnp.concatenate(rows, axis=0)

    @pl.when(j == nj - 1)
    def _():
        ki = tail_ref[:, 0:D_IDX]
        logit_self = jnp.sum(qi * ki, axis=-1, keepdims=True) * (D_IDX ** -0.5)
        s_self = jnp.sum(jnp.maximum(logit_self, 0.0) * wi, axis=0, keepdims=True)
        key_self = _sort_key(s_self)
        key = _sort_key(sc_ref[...])

        def total(x):
            return jnp.sum(jnp.sum(x, axis=0, keepdims=True), axis=1, keepdims=True)

        def count(pred):
            return total(pred(key).astype(jnp.int32)) + pred(key_self).astype(jnp.int32)

        thr = _kth_largest_key(lambda cand: count(lambda k: k >= cand), kk, (1, 1))
        need = (kk - count(lambda k: k > thr)).astype(F32)
        eq = key == thr
        eqb = eq.astype(BF16)
        in_page = (lax.broadcasted_iota(jnp.int32, (page, page), 0)
                   <= lax.broadcasted_iota(jnp.int32, (page, page), 1)).astype(BF16)
        earlier = (lax.broadcasted_iota(jnp.int32, (n_pages, n_pages), 1)
                   < lax.broadcasted_iota(jnp.int32, (n_pages, n_pages), 0)).astype(F32)
        per_page = jnp.broadcast_to(jnp.sum(eq.astype(F32), axis=1, keepdims=True), (n_pages, page))
        before = _dot(earlier.astype(BF16), per_page.astype(BF16))[:, 0:1]
        cum = _dot(eqb, in_page) + before
        sel = (key > thr) | (eq & (cum <= need))
        expanded = _dot(sel.astype(BF16), expand_ref[...])
        bias_ref[...] = jnp.where(expanded > 0.5, 0.0, NEG_INF).astype(F32)
        n_eq = total(eq.astype(F32))
        sel_self = (key_self > thr) | ((key_self == thr) & (n_eq + 1.0 <= need))
        bias_self_ref[...] = jnp.broadcast_to(jnp.where(sel_self, 0.0, NEG_INF).astype(F32), (1, LANE))


def _idxsel_call(page_table_flat, qi8, wi8, tail3, expand, cache_idx, layer, n_pages, pages, kk):
    bs = qi8.shape[0]
    page = cache_idx.shape[2]

    def page_spec(k):
        return pl.BlockSpec((None, None, page, D_IDX),
                            lambda s, j, pt: (layer, pt[s * n_pages + j * pages + k], 0, 0))

    per_seq = lambda shape: pl.BlockSpec((None,) + shape, lambda s, j, pt: (s,) + (0,) * len(shape))
    return pl.pallas_call(
        functools.partial(_idxsel_kernel, pages=pages, kk=kk),
        grid_spec=pltpu.PrefetchScalarGridSpec(
            num_scalar_prefetch=1,
            grid=(bs, n_pages // pages),
            in_specs=[per_seq((8, D_IDX)), per_seq((8, LANE)), per_seq((1, TAILW)),
                      pl.BlockSpec((page, page * N_HEADS), lambda s, j, pt: (0, 0))]
                     + [page_spec(k) for k in range(pages)],
            out_specs=[per_seq((n_pages, page * N_HEADS)), per_seq((1, LANE))],
            scratch_shapes=[pltpu.VMEM((n_pages, page), F32)]),
        out_shape=[jax.ShapeDtypeStruct((bs, n_pages, page * N_HEADS), F32),
                   jax.ShapeDtypeStruct((bs, 1, LANE), F32)],
        compiler_params=_cparams(("arbitrary", "arbitrary"), VMEM_LIMIT),
        name="decode_index_select",
    )(page_table_flat, qi8, wi8, tail3, expand, *([cache_idx] * pages))


def _dec_kernel(pt_ref, q_ref, kvn_ref, hmask_ref, bias_ref, bself_ref, *rest, pages, pages_per_block):
    page_refs = rest[:pages]
    fin_ref, pm_ref, pl_ref, pa_ref, xb_ref, s_ref, p_ref, m_ref, l_ref, acc_ref = rest[pages:]
    j = pl.program_id(1)
    nj = pl.num_programs(1)
    row = lax.broadcasted_iota(jnp.int32, (DEC_ROWS, 1), 0)
    is_dsa = row >= 24
    is_moba = (row >= 16) & (row < 24)
    scale = jnp.where(row < 16, DQK_A ** -0.5, HEAD_DIM ** -0.5).astype(F32)
    qf = q_ref[...] * scale
    qb = qf.astype(BF16)

    @pl.when(j == 0)
    def _():
        kvn = kvn_ref[...]
        s_self = jnp.sum(qf * kvn, axis=-1, keepdims=True) + jnp.where(is_dsa, bself_ref[:, 0:1], 0.0)
        m0 = jnp.maximum(s_self, M_INIT)
        p0 = jnp.exp(s_self - m0)
        m_ref[...] = jnp.broadcast_to(m0, m_ref.shape)
        l_ref[...] = jnp.broadcast_to(p0, l_ref.shape)
        acc_ref[...] = p0 * kvn

    blocks = pages // pages_per_block
    for k in range(pages):
        xb = page_refs[k][...].astype(BF16)
        xb_ref[k] = xb
        s_ref[k] = _dot(qb, xb, _NT) + hmask_ref[...] + jnp.where(is_dsa, bias_ref[k:k + 1, :], 0.0)
    block_max = []
    for kb in range(blocks):
        mx = jnp.max(s_ref[kb * pages_per_block], axis=-1, keepdims=True)
        for kp in range(1, pages_per_block):
            mx = jnp.maximum(mx, jnp.max(s_ref[kb * pages_per_block + kp], axis=-1, keepdims=True))
        block_max.append(mx)
    m_old = m_ref[:, 0:1]
    m_all = m_old
    for mx in block_max:
        m_all = jnp.maximum(m_all, mx)
    alpha = jnp.exp(m_old - m_all)
    block_m, block_l = [], []
    for kb in range(blocks):
        m_use = jnp.where(is_moba, jnp.maximum(block_max[kb], M_INIT), m_all)
        lb = jnp.zeros((DEC_ROWS, 1), F32)
        for kp in range(pages_per_block):
            k = kb * pages_per_block + kp
            p = jnp.exp(s_ref[k] - m_use)
            lb = lb + jnp.sum(p, axis=-1, keepdims=True)
            p_ref[k] = p.astype(BF16)
        block_m.append(m_use)
        block_l.append(lb)
    l_new = alpha * l_ref[:, 0:1]
    acc_new = alpha * acc_ref[...]
    for kb in range(blocks):
        accb = _dot(p_ref[kb * pages_per_block], xb_ref[kb * pages_per_block])
        for kp in range(1, pages_per_block):
            k = kb * pages_per_block + kp
            accb = accb + _dot(p_ref[k], xb_ref[k])
        l_new = l_new + block_l[kb]
        acc_new = acc_new + accb
        blk = j * blocks + kb
        pm_ref[pl.ds(blk, 1)] = jnp.broadcast_to(block_m[kb][16:24], (8, LANE)).reshape(1, 8, LANE)
        pl_ref[pl.ds(blk, 1)] = jnp.broadcast_to(block_l[kb][16:24], (8, LANE)).reshape(1, 8, LANE)
        pa_ref[pl.ds(blk, 1)] = accb[16:24].reshape(1, 8, ROW)
    m_ref[...] = jnp.broadcast_to(m_all, m_ref.shape)
    l_ref[...] = jnp.broadcast_to(l_new, l_ref.shape)
    acc_ref[...] = acc_new

    @pl.when(j == nj - 1)
    def _():
        fin_ref[...] = acc_ref[...] / l_ref[:, 0:1]


def _dec_call(page_table_flat, q_dec, kvn32, hmask, bias, bias_self, cache_rows, layer, n_pages, pages,
              pages_per_block):
    bs = q_dec.shape[0]
    rows_per_page = cache_rows.shape[2]
    nblk = n_pages // pages_per_block

    def page_spec(k):
        return pl.BlockSpec((None, None, rows_per_page, ROW),
                            lambda s, j, pt: (layer, pt[s * n_pages + j * pages + k], 0, 0))

    per_seq = lambda shape: pl.BlockSpec((None,) + shape, lambda s, j, pt: (s,) + (0,) * len(shape))
    return pl.pallas_call(
        functools.partial(_dec_kernel, pages=pages, pages_per_block=pages_per_block),
        grid_spec=pltpu.PrefetchScalarGridSpec(
            num_scalar_prefetch=1,
            grid=(bs, n_pages // pages),
            in_specs=[per_seq((DEC_ROWS, ROW)), per_seq((DEC_ROWS, ROW)),
                      pl.BlockSpec((DEC_ROWS, rows_per_page), lambda s, j, pt: (0, 0)),
                      pl.BlockSpec((None, pages, rows_per_page), lambda s, j, pt: (s, j, 0)),
                      per_seq((1, LANE))] + [page_spec(k) for k in range(pages)],
            out_specs=[per_seq((DEC_ROWS, ROW)), per_seq((nblk, 8, LANE)), per_seq((nblk, 8, LANE)),
                       per_seq((nblk, 8, ROW))],
            scratch_shapes=[pltpu.VMEM((pages, rows_per_page, ROW), BF16),
                            pltpu.VMEM((pages, DEC_ROWS, rows_per_page), F32),
                            pltpu.VMEM((pages, DEC_ROWS, rows_per_page), BF16),
                            pltpu.VMEM((DEC_ROWS, LANE), F32), pltpu.VMEM((DEC_ROWS, LANE), F32),
                            pltpu.VMEM((DEC_ROWS, ROW), F32)]),
        out_shape=[jax.ShapeDtypeStruct((bs, DEC_ROWS, ROW), F32),
                   jax.ShapeDtypeStruct((bs, nblk, 8, LANE), F32),
                   jax.ShapeDtypeStruct((bs, nblk, 8, LANE), F32),
                   jax.ShapeDtypeStruct((bs, nblk, 8, ROW), F32)],
        compiler_params=_cparams(("arbitrary", "arbitrary"), VMEM_LIMIT),
        name="decode_paged_attention",
    )(page_table_flat, q_dec, kvn32, hmask, bias, bias_self, *([cache_rows] * pages))


def _fin_kernel(lam_ref, fin_ref, pm_ref, pl_ref, pa_ref, q_ref, kvn_ref, g_ref,
                oa_ref, ob_ref, oc_ref, *, nblk, norm_scale):
    lam = lam_ref[0]
    for h in range(H_A):
        o = fin_ref[2 * h:2 * h + 1, HEAD_DIM:ROW] - lam * fin_ref[2 * h + 1:2 * h + 2, HEAD_DIM:ROW]
        o = o * lax.rsqrt(jnp.mean(o * o, axis=-1, keepdims=True) + LN_EPS)
        oa_ref[:, h * HEAD_DIM:(h + 1) * HEAD_DIM] = (o * g_ref[...] * norm_scale).astype(oa_ref.dtype)
    for h in range(H_C):
        oc_ref[:, h * HEAD_DIM:(h + 1) * HEAD_DIM] = fin_ref[24 + h:25 + h, HEAD_DIM:ROW].astype(oc_ref.dtype)

    blk = lax.broadcasted_iota(jnp.int32, (nblk, 1), 0)
    ksel = min(MOBA_TOPK, nblk)
    for h in range(H_B):
        qh = q_ref[:, Q_QB + h * HEAD_DIM:Q_QB + (h + 1) * HEAD_DIM]
        mean = pa_ref[:, H_B + h, 0:HEAD_DIM] * (1.0 / MOBA_BLOCK)
        gate = jnp.sum(mean * qh, axis=-1, keepdims=True)
        sel = jnp.zeros((nblk, 1), jnp.bool_)
        for _ in range(ksel):
            best = jnp.max(gate, axis=0, keepdims=True)
            first = jnp.min(jnp.where(gate == best, blk, nblk), axis=0, keepdims=True)
            pick = blk == first
            sel = sel | pick
            gate = jnp.where(pick, -jnp.inf, gate)
        k_self = kvn_ref[H_A + h:H_A + h + 1, 0:HEAD_DIM]
        v_self = kvn_ref[H_A + h:H_A + h + 1, HEAD_DIM:ROW]
        s_self = jnp.sum(qh * k_self, axis=-1, keepdims=True) * (HEAD_DIM ** -0.5)
        m_b = pm_ref[:, h, 0:1]
        l_b = pl_ref[:, h, 0:1]
        m_star = jnp.maximum(jnp.max(jnp.where(sel, m_b, M_INIT), axis=0, keepdims=True), s_self)
        w_b = jnp.where(sel, jnp.exp(m_b - m_star), 0.0)
        w_self = jnp.exp(s_self - m_star)
        l_star = jnp.sum(w_b * l_b, axis=0, keepdims=True) + w_self
        acc = jnp.sum(w_b * pa_ref[:, h, HEAD_DIM:ROW], axis=0, keepdims=True) + w_self * v_self
        ob_ref[:, h * HEAD_DIM:(h + 1) * HEAD_DIM] = (acc / l_star).astype(ob_ref.dtype)


def _fin_call(lam, fin, pm, pl_, pa, q_s, kvn16, g, norm_scale):
    bs, nblk = pm.shape[0], pm.shape[1]
    per_seq = lambda a: pl.BlockSpec((None,) + a.shape[1:], lambda s: (s,) + (0,) * (a.ndim - 1))
    q3 = q_s.reshape(bs, 1, QW)
    out = lambda w: pl.BlockSpec((None, 1, w), lambda s: (s, 0, 0))
    return pl.pallas_call(
        functools.partial(_fin_kernel, nblk=nblk, norm_scale=norm_scale),
        grid=(bs,),
        in_specs=[pl.BlockSpec(memory_space=pltpu.SMEM), per_seq(fin), per_seq(pm), per_seq(pl_), per_seq(pa),
                  per_seq(q3), per_seq(kvn16), pl.BlockSpec((1, HEAD_DIM), lambda s: (0, 0))],
        out_specs=[out(H_A * HEAD_DIM), out(H_B * HEAD_DIM), out(H_C * HEAD_DIM)],
        out_shape=[jax.ShapeDtypeStruct((bs, 1, H_A * HEAD_DIM), BF16),
                   jax.ShapeDtypeStruct((bs, 1, H_B * HEAD_DIM), BF16),
                   jax.ShapeDtypeStruct((bs, 1, H_C * HEAD_DIM), BF16)],
        compiler_params=_cparams(("arbitrary",)),
        name="decode_finish",
    )(lam, fin, pm, pl_, pa, q3, kvn16, g)


def kernel(x_prompt, x_sample, cache_kv, cache_index_k, page_table, state_ffn_conv, c_prompt, c_sample, w_in, w_out, diff_lambda, diff_norm_g, w_ada_attn, b_ada_attn, ln1_g, ln1_b, w_ada_ffn, b_ada_ffn, w_up, w_gate, conv_w, conv_b, w_down, ln2_g, ln2_b):
    b, t, d = x_prompt.shape
    bs, ts, _ = x_sample.shape
    depth, n_pool, page, n_heads, row_w = cache_kv.shape
    n_pages = page_table.shape[1]
    past = n_pages * page
    assert ts == 1 and n_heads == N_HEADS and row_w == ROW and d == N_HEADS * HEAD_DIM
    assert conv_w.shape[1] == 3 and MOBA_BLOCK % page == 0 and past % MOBA_BLOCK == 0
    assert t % MOBA_BLOCK == 0 and bs % SUBLANE == 0 and page == LANE
    alpha = (2 * depth) ** 0.25
    pages_per_block = MOBA_BLOCK // page
    tq = MOBA_BLOCK
    tm_proj = min(512, t)
    tm_ffn = min(256, t)
    topk_prompt = min(DSA_TOPK, t // 4)
    n_keys_s = past + ts
    kk_sample = min(min(DSA_TOPK, n_keys_s // 4), n_keys_s)
    dec_pages = math.gcd(n_pages, 8)
    assert dec_pages % pages_per_block == 0 and dec_pages % SUBLANE == 0

    c_all = jnp.concatenate([c_prompt, c_sample], axis=0)
    m_attn = _ada_call(c_all, w_ada_attn, b_ada_attn)
    m_ffn = _ada_call(c_all, w_ada_ffn, b_ada_ffn)

    def mods(m, l):
        parts = []
        for k in range(3):
            mk = m[l, :, k * d:(k + 1) * d]
            parts.append((mk[:b].reshape(b, 1, d), mk[b:].reshape(1, bs, d)))
        return parts

    tabs_p = _rope_tables(jnp.arange(t, dtype=jnp.int32))
    tabs_s = _rope_tables(jnp.full((bs,), past, jnp.int32))
    pt_flat = page_table.reshape(-1).astype(jnp.int32)
    cache_rows = cache_kv.reshape(depth, n_pool, page * N_HEADS, ROW)
    dec_qidx = jnp.asarray(_DEC_QIDX)
    head_of_row = jnp.asarray(DEC_HEAD_OF_ROW)
    lane_head = np.arange(page * N_HEADS) % N_HEADS
    hmask = jnp.asarray(np.where(lane_head[None, :] == DEC_HEAD_OF_ROW[:, None], 0.0, NEG_INF).astype(np.float32))
    expand = jnp.asarray((np.arange(page * N_HEADS)[None, :] // N_HEADS == np.arange(page)[:, None])
                         .astype(np.float32)).astype(BF16)

    xp = x_prompt
    xs = x_sample.reshape(1, bs, d)
    kvp, kvs, ikp, iks, cvp, cvs = [], [], [], [], [], []
    for l in range(depth):
        dl = diff_lambda[l].astype(F32)
        lam_init = 0.8 - 0.6 * math.exp(-0.3 * l)
        lam = (jnp.exp(jnp.sum(dl[0] * dl[1])) - jnp.exp(jnp.sum(dl[2] * dl[3])) + lam_init).reshape(1)
        norm_scale = 1.0 - lam_init
        wl = w_in[l]
        wp = jnp.concatenate([wl[:, a:a + n] for a, n in _PERM_SEGMENTS]
                             + [jnp.zeros((d, WP - D_IN), F32)], axis=1).astype(BF16)
        wo = w_out[l].astype(BF16)
        wu, wg, wd = w_up[l].astype(BF16), w_gate[l].astype(BF16), w_down[l].astype(BF16)
        gnorm = diff_norm_g[l].reshape(1, HEAD_DIM)
        (shift_p, shift_s), (scale_p, scale_s), (gate_p, gate_s) = mods(m_attn, l)
        (fshift_p, fshift_s), (fscale_p, fscale_s), (fgate_p, fgate_s) = mods(m_ffn, l)

        q, kv, kva, kvb, kvc, tail = _proj_call(xp, shift_p, scale_p, wp, tabs_p, tm_proj)
        oa = _diff_call(lam, q, kva, gnorm.reshape(HEAD_DIM, 1), tq, norm_scale)
        ob = _moba_call(q, kvb)
        oc = _dsa_call(q, tail, kvc, tq, topk_prompt)
        x1 = _out_call(oa, ob, oc, xp, gate_p, wo, ln1_g[l], ln1_b[l], tm_proj, alpha)
        xp, conv_p = _ffn_seq_call(x1, fshift_p, fscale_p, fgate_p, wu, wg, conv_w[l], conv_b[l], wd,
                                   ln2_g[l], ln2_b[l], tm_ffn, alpha)
        kvp.append(kv.reshape(b, t, N_HEADS, ROW))
        ikp.append(tail[:, :, :D_IDX])
        cvp.append(conv_p)

        q_s, kv_s, _, _, _, tail_s = _proj_call(xs, shift_s, scale_s, wp, tabs_s, bs)
        q_s, kv_s, tail_s = q_s[0], kv_s[0], tail_s[0]
        qi8 = jnp.pad(q_s[:, Q_QI:Q_QI + H_IDX * D_IDX].reshape(bs, H_IDX, D_IDX), ((0, 0), (0, 4), (0, 0)))
        wi8 = jnp.broadcast_to(jnp.pad(tail_s[:, D_IDX:D_IDX + H_IDX], ((0, 0), (0, 4)))[:, :, None],
                               (bs, 8, LANE))
        bias, bias_self = _idxsel_call(pt_flat, qi8, wi8, tail_s.reshape(bs, 1, TAILW), expand, cache_index_k,
                                       l, n_pages, dec_pages, kk_sample)
        q_dec = jnp.concatenate([q_s, jnp.zeros((bs, 1), F32)], axis=1)[:, dec_qidx].reshape(bs, DEC_ROWS, ROW)
        kvn16 = kv_s.reshape(bs, N_HEADS, ROW)
        kvn32 = kvn16[:, head_of_row]
        fin, pm, pl_, pa = _dec_call(pt_flat, q_dec, kvn32, hmask, bias, bias_self, cache_rows, l, n_pages,
                                     dec_pages, pages_per_block)
        oa_s, ob_s, oc_s = _fin_call(lam, fin, pm, pl_, pa, q_s, kvn16, gnorm, norm_scale)
        x1s = _out_call(oa_s.reshape(1, bs, -1), ob_s.reshape(1, bs, -1), oc_s.reshape(1, bs, -1), xs, gate_s,
                        wo, ln1_g[l], ln1_b[l], bs, alpha)
        hist = state_ffn_conv[l]
        x2s, g_s = _ffn_rows_call(x1s[0], fshift_s[0], fscale_s[0], fgate_s[0], hist[:, 0], hist[:, 1],
                                  wu, wg, conv_w[l], conv_b[l], wd, ln2_g[l], ln2_b[l], alpha)
        xs = x2s.reshape(1, bs, d)
        kvs.append(kv_s.reshape(bs, ts, N_HEADS, ROW))
        iks.append(tail_s[:, :D_IDX].reshape(bs, ts, D_IDX))
        cvs.append(jnp.stack([hist[:, 1], g_s], axis=1))

    return (xp, xs.reshape(bs, ts, d), jnp.stack(kvp), jnp.stack(kvs), jnp.stack(ikp), jnp.stack(iks),
            jnp.stack(cvp), jnp.stack(cvs))
```

```python
import functools
import math

import jax
import jax.numpy as jnp
import numpy as np
from jax import lax
from jax.experimental import pallas as pl
from jax.experimental.pallas import tpu as pltpu

F32 = jnp.float32
BF16 = jnp.bfloat16

HEAD_DIM = 64
N_HEADS = 16
H_B = 4
H_A = 6
H_C = 6
DQK_A = 32
ROW = 2 * HEAD_DIM
H_IDX = 4
D_IDX = 64
ROPE_THETA = 500000.0
MOBA_BLOCK = 256
MOBA_TOPK = 3
DSA_TOPK = 256
Q_BLOCK = 128
LN_EPS = 1e-5
NEG_INF = -1e30
M_INIT = -1e20
INT_MIN = -2 ** 31

OFF_QA, OFF_KA, OFF_VA = 0, 384, 768
OFF_QB, OFF_KB, OFF_VB = 1152, 1408, 1664
OFF_QC, OFF_KC, OFF_VC = 1920, 2304, 2688
OFF_QI, OFF_KI, OFF_WI = 3072, 3328, 3392
D_IN = 3396

LANE = 128
SUBLANE = 8
QW = 1280
Q_QA, Q_QC, Q_QB, Q_QI = 0, 384, 768, 1024
KVW = N_HEADS * ROW
TAILW = 128
WP = QW + KVW + TAILW
VMEM_LIMIT = 56 * 1024 * 1024

PAT_QA, PAT_QB, PAT_KA, PAT_KB, PAT_TAIL = 0, 1, 2, 3, 4
PAT_HALF = (4, 8, 4, 8, 8)


def _perm_segments():
    seg = [(OFF_QA, 384), (OFF_QC, 384), (OFF_QB, 256), (OFF_QI, 256)]
    for h in range(H_A):
        seg += [(OFF_KA + h * 64, 64), (OFF_VA + h * 64, 64)]
    for h in range(H_B):
        seg += [(OFF_KB + h * 64, 64), (OFF_VB + h * 64, 64)]
    for h in range(H_C):
        seg += [(OFF_KC + h * 64, 64), (OFF_VC + h * 64, 64)]
    seg += [(OFF_KI, 64), (OFF_WI, 4)]
    return seg


_PERM_SEGMENTS = _perm_segments()
_GROUP_PAT = ([PAT_QA] * 3 + [PAT_QB] * 7 + [PAT_KA] * H_A + [PAT_KB] * (H_B + H_C) + [PAT_TAIL])

DEC_ROWS = 32
DEC_HEAD_OF_ROW = np.zeros((DEC_ROWS,), np.int32)
for _h in range(H_A):
    DEC_HEAD_OF_ROW[2 * _h] = DEC_HEAD_OF_ROW[2 * _h + 1] = _h
for _h in range(H_B):
    DEC_HEAD_OF_ROW[16 + _h] = DEC_HEAD_OF_ROW[20 + _h] = H_A + _h
for _h in range(H_C):
    DEC_HEAD_OF_ROW[24 + _h] = H_A + H_B + _h


def _decode_q_index():
    idx = np.full((DEC_ROWS, ROW), QW, np.int32)
    for h in range(H_A):
        for c in range(2):
            idx[2 * h + c, c * 32:c * 32 + 32] = np.arange(32) + Q_QA + h * 64 + c * 32
    for h in range(H_B):
        idx[16 + h, 0:64] = np.arange(64) + Q_QB + h * 64
    for h in range(H_C):
        idx[24 + h, 0:64] = np.arange(64) + Q_QC + h * 64
    return idx.reshape(-1)


_DEC_QIDX = _decode_q_index()


def _split(x):
    hi = x.astype(BF16)
    lo = (x - hi.astype(F32)).astype(BF16)
    return hi, lo


_NN = (((1,), (0,)), ((), ()))
_NT = (((1,), (1,)), ((), ()))


def _dot(a, b, dims=_NN):
    return lax.dot_general(a, b, dims, preferred_element_type=F32)


def _dot3(a, b, dims=_NN):
    ah, al = _split(a)
    bh, bl = _split(b)
    return _dot(ah, bh, dims) + (_dot(ah, bl, dims) + _dot(al, bh, dims))


def _layer_norm(x, g, b):
    mu = jnp.mean(x, axis=-1, keepdims=True)
    xc = x - mu
    var = jnp.mean(xc * xc, axis=-1, keepdims=True)
    return xc * lax.rsqrt(var + LN_EPS) * g + b


def _cparams(sem, vmem=None):
    return pltpu.CompilerParams(dimension_semantics=sem, vmem_limit_bytes=vmem)


def _ada_kernel(c_ref, w_ref, b_ref, o_ref):
    c = c_ref[...]
    a = c / (1.0 + jnp.exp(-c))
    o_ref[...] = _dot3(a, w_ref[...]) + b_ref[...]


def _ada_call(c_all, w, b):
    depth, d, d3 = w.shape
    n = c_all.shape[0]
    return pl.pallas_call(
        _ada_kernel,
        grid=(depth, d3 // d),
        in_specs=[pl.BlockSpec((n, d), lambda l, j: (0, 0)),
                  pl.BlockSpec((None, d, d), lambda l, j: (l, 0, j)),
                  pl.BlockSpec((None, 1, d), lambda l, j: (l, 0, j))],
        out_specs=pl.BlockSpec((None, n, d), lambda l, j: (l, 0, j)),
        out_shape=jax.ShapeDtypeStruct((depth, n, d3), F32),
        compiler_params=_cparams(("arbitrary", "arbitrary")),
        name="ada_modulation",
    )(c_all, w, b.reshape(depth, 1, d3))


def _rope_tables(pos):
    posf = pos.astype(F32)[:, None]
    r = pos.shape[0]

    def family(half, group):
        inv = jnp.power(ROPE_THETA, -jnp.arange(half, dtype=F32) / half)
        ang = posf * inv[None, :]
        cos, sin = jnp.cos(ang), jnp.sin(ang)
        rest = group - 2 * half
        cg = jnp.concatenate([cos, cos, jnp.ones((r, rest), F32)], axis=-1)
        sg = jnp.concatenate([-sin, sin, jnp.zeros((r, rest), F32)], axis=-1)
        return cg, sg

    ca, sa = family(4, 32)
    cb, sb = family(8, 64)
    one64, zero64 = jnp.ones((r, 64), F32), jnp.zeros((r, 64), F32)
    tail_c = jnp.concatenate([jnp.full((r, 4), H_IDX ** -0.5, F32), jnp.ones((r, 60), F32)], axis=-1)
    tabs = [jnp.tile(ca, (1, 4)), jnp.tile(sa, (1, 4)),
            jnp.tile(cb, (1, 2)), jnp.tile(sb, (1, 2)),
            jnp.concatenate([ca, ca, one64], -1), jnp.concatenate([sa, sa, zero64], -1),
            jnp.concatenate([cb, one64], -1), jnp.concatenate([sb, zero64], -1),
            jnp.concatenate([cb, tail_c], -1), jnp.concatenate([sb, zero64], -1)]
    return jnp.stack(tabs)


def _proj_kernel(x_ref, shift_ref, scale_ref, w_ref, tab_ref,
                 q_ref, kv_ref, kva_ref, kvb_ref, kvc_ref, tail_ref):
    h = x_ref[...] * (1.0 + scale_ref[...]) + shift_ref[...]
    hb = h.astype(BF16)
    lane = lax.broadcasted_iota(jnp.int32, (1, LANE), 1)
    x1_mask = {4: (lane % 32) < 4, 8: (lane % 64) < 8}

    def rope(y, g):
        pat = _GROUP_PAT[g]
        half = PAT_HALF[pat]
        partner = jnp.where(x1_mask[half], pltpu.roll(y, LANE - half, 1), pltpu.roll(y, half, 1))
        return y * tab_ref[2 * pat] + partner * tab_ref[2 * pat + 1]

    n_groups = WP // LANE
    g = 0
    while g < n_groups:
        width = min(2, n_groups - g)
        y = _dot(hb, w_ref[:, g * LANE:(g + width) * LANE])
        for k in range(width):
            gg = g + k
            r = rope(y[:, k * LANE:(k + 1) * LANE], gg)
            c0 = gg * LANE
            if c0 < QW:
                q_ref[:, c0:c0 + LANE] = r
            elif c0 < QW + KVW:
                c = c0 - QW
                head = c // ROW
                kv_ref[pl.ds(head, x_ref.shape[0], stride=N_HEADS), :] = r
                rb = r.astype(BF16)
                if head < H_A:
                    kva_ref[:, c:c + LANE] = rb
                elif head < H_A + H_B:
                    kvb_ref[:, c - H_A * ROW:c - H_A * ROW + LANE] = rb
                else:
                    cc = c - (H_A + H_B) * ROW
                    kvc_ref[:, cc:cc + LANE] = rb
            else:
                tail_ref[...] = r
        g += width


def _proj_call(x, shift, scale, wp, tabs, tm):
    g_n, tg, d = x.shape
    rm = shift.shape[1]
    nt = tg // tm
    if rm == 1:
        mod_spec = pl.BlockSpec((None, 1, d), lambda i, g: (g, 0, 0))
    else:
        mod_spec = pl.BlockSpec((None, tm, d), lambda i, g: (g, i, 0))

    def row_spec(w):
        return pl.BlockSpec((None, tm, w), lambda i, g: (g, i, 0))

    widths = (QW, KVW, H_A * ROW, H_B * ROW, H_C * ROW, TAILW)
    dtypes = (F32, F32, BF16, BF16, BF16, F32)
    return pl.pallas_call(
        _proj_kernel,
        grid=(nt, g_n),
        in_specs=[row_spec(d), mod_spec, mod_spec,
                  pl.BlockSpec((d, WP), lambda i, g: (0, 0)),
                  pl.BlockSpec((10, tm, LANE), lambda i, g: (0, i, 0))],
        out_specs=[pl.BlockSpec((None, tm * N_HEADS, ROW), lambda i, g: (g, i, 0)) if k == 1 else row_spec(w)
                   for k, w in enumerate(widths)],
        out_shape=[jax.ShapeDtypeStruct((g_n, tg * N_HEADS, ROW) if k == 1 else (g_n, tg, w), dt)
                   for k, (w, dt) in enumerate(zip(widths, dtypes))],
        compiler_params=_cparams(("arbitrary", "arbitrary"), VMEM_LIMIT),
        name="in_proj_rope",
    )(x, shift, scale, wp, tabs)


def _chain_reset(m_ref, l_ref, acc_ref):
    m_ref[...] = jnp.full(m_ref.shape, M_INIT, F32)
    l_ref[...] = jnp.zeros(l_ref.shape, F32)
    acc_ref[...] = jnp.zeros(acc_ref.shape, F32)


def _chains_update(scores, values, s_ref, p_ref, a_ref, m_ref, l_ref, acc_ref):
    n_chains = len(scores)
    for n in range(n_chains):
        s_ref[n] = scores[n]()
    for n in range(n_chains):
        r = n * SUBLANE
        s = s_ref[n]
        m_old = m_ref[r:r + 1, :]
        m_new = jnp.maximum(m_old, jnp.max(s, axis=0, keepdims=True))
        alpha = jnp.exp(m_old - m_new)
        p = jnp.exp(s - m_new)
        l_ref[r:r + 1, :] = alpha * l_ref[r:r + 1, :] + jnp.sum(p, axis=0, keepdims=True)
        p_ref[n] = p.astype(BF16)
        a_ref[r:r + 1, :] = alpha
        m_ref[r:r + 1, :] = m_new
    for n in range(n_chains):
        vt = values[n]()
        dv = vt.shape[0]
        acc_ref[n * dv:(n + 1) * dv, :] = (a_ref[n * SUBLANE:n * SUBLANE + 1, :] * acc_ref[n * dv:(n + 1) * dv, :]
                                           + _dot(vt, p_ref[n]))


def _chain_scratch(chains, tq, dv):
    return [pltpu.VMEM((chains, tq, tq), F32), pltpu.VMEM((chains, tq, tq), BF16),
            pltpu.VMEM((chains * SUBLANE, tq), F32), pltpu.VMEM((chains * SUBLANE, tq), F32),
            pltpu.VMEM((chains * SUBLANE, tq), F32), pltpu.VMEM((chains * dv, tq), F32)]


def _chain_result(n, dv, l_ref, acc_ref):
    return acc_ref[n * dv:(n + 1) * dv, :] / l_ref[n * SUBLANE:n * SUBLANE + 1, :]


def _causal_bias_t(size):
    kpos = lax.broadcasted_iota(jnp.int32, (size, size), 0)
    qpos = lax.broadcasted_iota(jnp.int32, (size, size), 1)
    return jnp.where(kpos <= qpos, 0.0, NEG_INF).astype(F32)


def _build_vt(kv_ref, vt_ref, n_heads, n_chunks, ck):
    for c in range(n_chunks):
        for h in range(n_heads):
            x = kv_ref[c * ck:(c + 1) * ck, h * ROW:(h + 1) * ROW].astype(F32)
            vt_ref[c, h * HEAD_DIM:(h + 1) * HEAD_DIM, :] = x.T[HEAD_DIM:ROW, :].astype(BF16)


def _diff_kernel(lam_ref, q_ref, kv_ref, g_ref, o_ref, vt_ref, qs_ref, s_ref, p_ref, a_ref, m_ref, l_ref, acc_ref,
                 *, tq, n_chunks, norm_scale):
    i = pl.program_id(1)
    lam = lam_ref[0]

    @pl.when(i == 0)
    def _():
        _build_vt(kv_ref, vt_ref, H_A, n_chunks, tq)

    for n in range(2 * H_A):
        qs_ref[n] = (q_ref[:, n * DQK_A:(n + 1) * DQK_A] * (DQK_A ** -0.5)).astype(BF16)
    _chain_reset(m_ref, l_ref, acc_ref)

    def update(j, extra):
        r0 = pl.multiple_of(j * tq, tq)

        def score(n):
            h, c = divmod(n, 2)
            k = kv_ref[pl.ds(r0, tq), h * ROW + c * DQK_A:h * ROW + (c + 1) * DQK_A]
            s = _dot(k, qs_ref[n], _NT)
            return s if extra is None else s + extra

        def value(n):
            h = n // 2
            return vt_ref[j, h * HEAD_DIM:(h + 1) * HEAD_DIM, :]

        _chains_update([functools.partial(score, n) for n in range(2 * H_A)],
                       [functools.partial(value, n) for n in range(2 * H_A)],
                       s_ref, p_ref, a_ref, m_ref, l_ref, acc_ref)

    def body(j, carry):
        update(j, None)
        return carry

    lax.fori_loop(0, i, body, 0)
    update(i, _causal_bias_t(tq))

    outs = []
    for h in range(H_A):
        o = (_chain_result(2 * h, HEAD_DIM, l_ref, acc_ref)
             - lam * _chain_result(2 * h + 1, HEAD_DIM, l_ref, acc_ref))
        o = o * lax.rsqrt(jnp.mean(o * o, axis=0, keepdims=True) + LN_EPS)
        outs.append(o * g_ref[...] * norm_scale)
    o_ref[...] = jnp.concatenate(outs, axis=0).T.astype(o_ref.dtype)


def _diff_call(lam, q, kva, g_col, tq, norm_scale):
    b, t, _ = q.shape
    n_chunks = t // tq
    chains = 2 * H_A
    return pl.pallas_call(
        functools.partial(_diff_kernel, tq=tq, n_chunks=n_chunks, norm_scale=norm_scale),
        grid=(b, n_chunks),
        in_specs=[pl.BlockSpec(memory_space=pltpu.SMEM),
                  pl.BlockSpec((None, tq, 384), lambda b_, i: (b_, i, Q_QA // 384)),
                  pl.BlockSpec((None, t, H_A * ROW), lambda b_, i: (b_, 0, 0)),
                  pl.BlockSpec((HEAD_DIM, 1), lambda b_, i: (0, 0))],
        out_specs=pl.BlockSpec((None, tq, H_A * HEAD_DIM), lambda b_, i: (b_, i, 0)),
        out_shape=jax.ShapeDtypeStruct((b, t, H_A * HEAD_DIM), BF16),
        scratch_shapes=[pltpu.VMEM((n_chunks, H_A * HEAD_DIM, tq), BF16),
                        pltpu.VMEM((chains, tq, DQK_A), BF16)] + _chain_scratch(chains, tq, HEAD_DIM),
        compiler_params=_cparams(("arbitrary", "arbitrary"), VMEM_LIMIT),
        name="diff_attention",
    )(lam, q, kva, g_col)


def _moba_kernel(q_ref, kv_ref, o_ref, mean_ref, vt_ref, qs_ref, gate_ref, s_ref, p_ref, a_ref, m_ref, l_ref,
                 acc_ref, *, n_blocks, nbr):
    i = pl.program_id(1)
    tq = MOBA_BLOCK

    @pl.when(i == 0)
    def _():
        mean_ref[...] = jnp.zeros_like(mean_ref)
        for blk in range(n_blocks):
            kblk = kv_ref[blk * tq:(blk + 1) * tq, :].astype(F32)
            mean_ref[blk:blk + 1, :] = jnp.mean(kblk, axis=0, keepdims=True)
        _build_vt(kv_ref, vt_ref, H_B, n_blocks, tq)

    for h in range(H_B):
        qf = q_ref[:, h * HEAD_DIM:(h + 1) * HEAD_DIM]
        qs_ref[h] = (qf * (HEAD_DIM ** -0.5)).astype(BF16)
        gate_ref[h * nbr:(h + 1) * nbr, :] = _dot3(mean_ref[:, h * ROW:h * ROW + HEAD_DIM], qf, _NT)
    _chain_reset(m_ref, l_ref, acc_ref)
    ksel = jnp.minimum(MOBA_TOPK, i)
    blk_id = lax.broadcasted_iota(jnp.int32, (nbr, 1), 0)

    def update(c, own):
        r0 = pl.multiple_of(c * tq, tq)

        def score(h):
            k = kv_ref[pl.ds(r0, tq), h * ROW:h * ROW + HEAD_DIM]
            s = _dot(k, qs_ref[h], _NT)
            if own:
                return s + _causal_bias_t(tq)
            gate = gate_ref[h * nbr:(h + 1) * nbr, :]
            col = jnp.sum(jnp.where(blk_id == c, gate, 0.0), axis=0, keepdims=True)
            beats = ((gate > col) | ((gate == col) & (blk_id < c))) & (blk_id < i)
            rank = jnp.sum(beats.astype(jnp.int32), axis=0, keepdims=True)
            return s + jnp.where(rank < ksel, 0.0, NEG_INF).astype(F32)

        def value(h):
            return vt_ref[c, h * HEAD_DIM:(h + 1) * HEAD_DIM, :]

        _chains_update([functools.partial(score, h) for h in range(H_B)],
                       [functools.partial(value, h) for h in range(H_B)],
                       s_ref, p_ref, a_ref, m_ref, l_ref, acc_ref)

    def body(c, carry):
        update(c, False)
        return carry

    lax.fori_loop(0, i, body, 0)
    update(i, True)
    outs = [_chain_result(h, HEAD_DIM, l_ref, acc_ref) for h in range(H_B)]
    o_ref[...] = jnp.concatenate(outs, axis=0).T.astype(o_ref.dtype)


def _moba_call(q, kvb):
    b, t, _ = q.shape
    tq = MOBA_BLOCK
    n_blocks = t // tq
    nbr = -(-n_blocks // SUBLANE) * SUBLANE
    return pl.pallas_call(
        functools.partial(_moba_kernel, n_blocks=n_blocks, nbr=nbr),
        grid=(b, n_blocks),
        in_specs=[pl.BlockSpec((None, tq, 256), lambda b_, i: (b_, i, Q_QB // 256)),
                  pl.BlockSpec((None, t, H_B * ROW), lambda b_, i: (b_, 0, 0))],
        out_specs=pl.BlockSpec((None, tq, H_B * HEAD_DIM), lambda b_, i: (b_, i, 0)),
        out_shape=jax.ShapeDtypeStruct((b, t, H_B * HEAD_DIM), BF16),
        scratch_shapes=[pltpu.VMEM((nbr, H_B * ROW), F32),
                        pltpu.VMEM((n_blocks, H_B * HEAD_DIM, tq), BF16),
                        pltpu.VMEM((H_B, tq, HEAD_DIM), BF16),
                        pltpu.VMEM((H_B * nbr, tq), F32)] + _chain_scratch(H_B, tq, HEAD_DIM),
        compiler_params=_cparams(("arbitrary", "arbitrary"), VMEM_LIMIT),
        name="moba_attention",
    )(q, kvb)


def _sort_key(score):
    bits = lax.bitcast_convert_type(jnp.where(score == 0.0, 0.0, score), jnp.int32)
    return jnp.where(bits < 0, bits ^ jnp.int32(0x7FFFFFFF), bits)


def _kth_largest_key(count_ge, kk, shape):
    def body(it, prefix):
        bit = lax.shift_left(jnp.int32(1), 31 - it)
        cand_u = prefix | bit
        cnt = count_ge(cand_u ^ jnp.int32(INT_MIN))
        return jnp.where(cnt >= kk, cand_u, prefix)

    prefix = lax.fori_loop(0, 32, body, jnp.zeros(shape, jnp.int32))
    return prefix ^ jnp.int32(INT_MIN)


def _dsa_kernel(q_ref, qi_ref, tq_ref, tk_ref, kv_ref, o_ref,
                key_ref, bias_ref, vt_ref, qs_ref, qih_ref, qil_ref, sel_ref, s_ref, p_ref, a_ref, m_ref, l_ref,
                acc_ref, *, tq, n_chunks_total, topk):
    i = pl.program_id(1)
    n_chunks = i + 1

    @pl.when(i == 0)
    def _():
        _build_vt(kv_ref, vt_ref, H_C, n_chunks_total, tq)

    for h in range(H_C):
        qs_ref[h] = (q_ref[:, h * HEAD_DIM:(h + 1) * HEAD_DIM] * (HEAD_DIM ** -0.5)).astype(BF16)
    for h in range(H_IDX):
        hi, lo = _split(qi_ref[:, h * D_IDX:(h + 1) * D_IDX])
        qih_ref[h] = hi
        qil_ref[h] = lo
    tail_t = tq_ref[...].T
    qpos = i * tq + lax.broadcasted_iota(jnp.int32, (1, tq), 1)
    kk = jnp.minimum(topk, ((qpos // Q_BLOCK) + 1) * Q_BLOCK)
    diag_bias = _causal_bias_t(tq)

    def score_chunk(j, masked):
        r0 = pl.multiple_of(j * tq, tq)
        ik_hi, ik_lo = _split(tk_ref[pl.ds(r0, tq), 0:D_IDX])
        score = jnp.zeros((tq, tq), F32)
        for h in range(H_IDX):
            logits = (_dot(ik_hi, qih_ref[h], _NT)
                      + (_dot(ik_hi, qil_ref[h], _NT) + _dot(ik_lo, qih_ref[h], _NT))) * (D_IDX ** -0.5)
            score = score + jnp.maximum(logits, 0.0) * tail_t[D_IDX + h:D_IDX + h + 1, :]
        if masked:
            score = jnp.where(diag_bias < 0.0, NEG_INF, score)
        key_ref[j] = _sort_key(score)

    def score_body(j, carry):
        score_chunk(j, False)
        return carry

    lax.fori_loop(0, i, score_body, 0)
    score_chunk(i, True)

    def count(pred):
        def body(j, cnt):
            hit = pred(key_ref[j]).astype(jnp.int32).reshape(tq // SUBLANE, SUBLANE, tq)
            return cnt + jnp.sum(hit, axis=0)
        cnt = lax.fori_loop(0, n_chunks, body, jnp.zeros((SUBLANE, tq), jnp.int32))
        return jnp.sum(cnt, axis=0, keepdims=True)

    @pl.when(n_chunks * tq <= topk)
    def _():
        sel_ref[0:1, :] = jnp.full((1, tq), INT_MIN, jnp.int32)
        sel_ref[SUBLANE:SUBLANE + 1, :] = jnp.full((1, tq), 2 ** 30, jnp.int32)

    @pl.when(n_chunks * tq > topk)
    def _():
        thr = _kth_largest_key(lambda cand: count(lambda k: k >= cand), kk, (1, tq))
        n_gt = count(lambda k: k > thr)
        sel_ref[0:1, :] = thr
        sel_ref[SUBLANE:SUBLANE + 1, :] = kk - n_gt

    thr = sel_ref[0:1, :]
    need = sel_ref[SUBLANE:SUBLANE + 1, :].astype(F32)
    tri = (lax.broadcasted_iota(jnp.int32, (tq, tq), 1)
           <= lax.broadcasted_iota(jnp.int32, (tq, tq), 0)).astype(BF16)

    def bias_body(j, seen):
        key = key_ref[j]
        eq = key == thr
        cum = seen + _dot(tri, eq.astype(BF16))
        sel = (key > thr) | (eq & (cum <= need))
        bias_ref[j] = jnp.where(sel, 0.0, NEG_INF).astype(F32)
        return seen + jnp.sum(eq.astype(F32), axis=0, keepdims=True)

    lax.fori_loop(0, n_chunks, bias_body, jnp.zeros((1, tq), F32))

    _chain_reset(m_ref, l_ref, acc_ref)

    def update(j, extra):
        r0 = pl.multiple_of(j * tq, tq)

        def score(h):
            k = kv_ref[pl.ds(r0, tq), h * ROW:h * ROW + HEAD_DIM]
            s = _dot(k, qs_ref[h], _NT) + bias_ref[j]
            return s if extra is None else s + extra

        def value(h):
            return vt_ref[j, h * HEAD_DIM:(h + 1) * HEAD_DIM, :]

        _chains_update([functools.partial(score, h) for h in range(H_C)],
                       [functools.partial(value, h) for h in range(H_C)],
                       s_ref, p_ref, a_ref, m_ref, l_ref, acc_ref)

    def body(j, carry):
        update(j, None)
        return carry

    lax.fori_loop(0, i, body, 0)
    update(i, diag_bias)
    outs = [_chain_result(h, HEAD_DIM, l_ref, acc_ref) for h in range(H_C)]
    o_ref[...] = jnp.concatenate(outs, axis=0).T.astype(o_ref.dtype)


def _dsa_call(q, tail, kvc, tq, topk):
    b, t, _ = q.shape
    n_chunks = t // tq
    return pl.pallas_call(
        functools.partial(_dsa_kernel, tq=tq, n_chunks_total=n_chunks, topk=topk),
        grid=(b, n_chunks),
        in_specs=[pl.BlockSpec((None, tq, 384), lambda b_, i: (b_, i, Q_QC // 384)),
                  pl.BlockSpec((None, tq, 256), lambda b_, i: (b_, i, Q_QI // 256)),
                  pl.BlockSpec((None, tq, TAILW), lambda b_, i: (b_, i, 0)),
                  pl.BlockSpec((None, t, TAILW), lambda b_, i: (b_, 0, 0)),
                  pl.BlockSpec((None, t, H_C * ROW), lambda b_, i: (b_, 0, 0))],
        out_specs=pl.BlockSpec((None, tq, H_C * HEAD_DIM), lambda b_, i: (b_, i, 0)),
        out_shape=jax.ShapeDtypeStruct((b, t, H_C * HEAD_DIM), BF16),
        scratch_shapes=[pltpu.VMEM((n_chunks, tq, tq), jnp.int32),
                        pltpu.VMEM((n_chunks, tq, tq), F32),
                        pltpu.VMEM((n_chunks, H_C * HEAD_DIM, tq), BF16),
                        pltpu.VMEM((H_C, tq, HEAD_DIM), BF16),
                        pltpu.VMEM((H_IDX, tq, D_IDX), BF16),
                        pltpu.VMEM((H_IDX, tq, D_IDX), BF16),
                        pltpu.VMEM((2 * SUBLANE, tq), jnp.int32)] + _chain_scratch(H_C, tq, HEAD_DIM),
        compiler_params=_cparams(("arbitrary", "arbitrary"), VMEM_LIMIT),
        name="dsa_attention",
    )(q, q, tail, tail, kvc)


def _out_kernel(oa_ref, ob_ref, oc_ref, x_ref, gate_ref, w_ref, g_ref, b_ref, y_ref, *, alpha):
    wa, wb = H_A * HEAD_DIM, (H_A + H_B) * HEAD_DIM
    o = (_dot(oa_ref[...], w_ref[0:wa, :]) + _dot(ob_ref[...], w_ref[wa:wb, :])
         + _dot(oc_ref[...], w_ref[wb:, :]))
    y = alpha * x_ref[...] + (1.0 + gate_ref[...]) * o
    y_ref[...] = _layer_norm(y, g_ref[...], b_ref[...])


def _out_call(oa, ob, oc, x, gate, w_out, g, b, tm, alpha):
    g_n, tg, d = x.shape
    rm = gate.shape[1]
    if rm == 1:
        mod_spec = pl.BlockSpec((None, 1, d), lambda g_, i: (g_, 0, 0))
    else:
        mod_spec = pl.BlockSpec((None, tm, d), lambda g_, i: (g_, i, 0))

    def row_spec(w):
        return pl.BlockSpec((None, tm, w), lambda g_, i: (g_, i, 0))

    vec = pl.BlockSpec((1, d), lambda g_, i: (0, 0))
    return pl.pallas_call(
        functools.partial(_out_kernel, alpha=alpha),
        grid=(g_n, tg // tm),
        in_specs=[row_spec(oa.shape[-1]), row_spec(ob.shape[-1]), row_spec(oc.shape[-1]), row_spec(d),
                  mod_spec, pl.BlockSpec((d, d), lambda g_, i: (0, 0)), vec, vec],
        out_specs=row_spec(d),
        out_shape=jax.ShapeDtypeStruct((g_n, tg, d), F32),
        compiler_params=_cparams(("arbitrary", "arbitrary"), VMEM_LIMIT),
        name="out_proj_ln",
    )(oa, ob, oc, x, gate, w_out, g.reshape(1, d), b.reshape(1, d))


def _ffn_tail(x, gate, u, gc, wd_ref, g_ref, b_ref, alpha):
    act = gc / (1.0 + jnp.exp(-gc)) * u
    f = _dot(act.astype(BF16), wd_ref[...])
    return _layer_norm(alpha * x + (1.0 + gate) * f, g_ref[...], b_ref[...])


def _ffn_seq_kernel(x_ref, shift_ref, scale_ref, gate_ref, wu_ref, wg_ref, cw_ref, cb_ref, wd_ref,
                    g_ref, b_ref, y_ref, conv_ref, carry_ref, *, alpha, tm):
    j = pl.program_id(1)

    @pl.when(j == 0)
    def _():
        carry_ref[...] = jnp.zeros_like(carry_ref)

    x = x_ref[...]
    hb = (x * (1.0 + scale_ref[...]) + shift_ref[...]).astype(BF16)
    u = _dot(hb, wu_ref[...])
    g = _dot(hb, wg_ref[...])
    row = lax.broadcasted_iota(jnp.int32, (tm, 1), 0)
    prev1, prev2 = carry_ref[1:2, :], carry_ref[0:1, :]
    g1 = jnp.where(row == 0, prev1, pltpu.roll(g, 1, 0))
    g2 = jnp.where(row == 0, prev2, jnp.where(row == 1, prev1, pltpu.roll(g, 2, 0)))
    gc = cb_ref[...] + g2 * cw_ref[0:1, :] + g1 * cw_ref[1:2, :] + g * cw_ref[2:3, :]
    last2 = g[tm - 2:tm, :]
    carry_ref[...] = last2
    conv_ref[...] = last2
    y_ref[...] = _ffn_tail(x, gate_ref[...], u, gc, wd_ref, g_ref, b_ref, alpha)


def _ffn_rows_kernel(x_ref, shift_ref, scale_ref, gate_ref, h0_ref, h1_ref, wu_ref, wg_ref, cw_ref, cb_ref,
                     wd_ref, g_ref, b_ref, y_ref, gout_ref, *, alpha):
    x = x_ref[...]
    hb = (x * (1.0 + scale_ref[...]) + shift_ref[...]).astype(BF16)
    u = _dot(hb, wu_ref[...])
    g = _dot(hb, wg_ref[...])
    gc = cb_ref[...] + h0_ref[...] * cw_ref[0:1, :] + h1_ref[...] * cw_ref[1:2, :] + g * cw_ref[2:3, :]
    gout_ref[...] = g
    y_ref[...] = _ffn_tail(x, gate_ref[...], u, gc, wd_ref, g_ref, b_ref, alpha)


def _ffn_seq_call(x, shift, scale, gate, wu, wg, cw, cb, wd, g, b, tm, alpha):
    g_n, tg, d = x.shape
    dff = wu.shape[1]
    mod_spec = pl.BlockSpec((None, 1, d), lambda g_, i: (g_, 0, 0))
    row_spec = pl.BlockSpec((None, tm, d), lambda g_, i: (g_, i, 0))
    const = lambda shape: pl.BlockSpec(shape, lambda g_, i: (0, 0))
    return pl.pallas_call(
        functools.partial(_ffn_seq_kernel, alpha=alpha, tm=tm),
        grid=(g_n, tg // tm),
        in_specs=[row_spec, mod_spec, mod_spec, mod_spec, const((d, dff)), const((d, dff)),
                  const((3, dff)), const((1, dff)), const((dff, d)), const((1, d)), const((1, d))],
        out_specs=[row_spec, pl.BlockSpec((None, 2, dff), lambda g_, i: (g_, 0, 0))],
        out_shape=[jax.ShapeDtypeStruct((g_n, tg, d), F32), jax.ShapeDtypeStruct((g_n, 2, dff), F32)],
        scratch_shapes=[pltpu.VMEM((2, dff), F32)],
        compiler_params=_cparams(("arbitrary", "arbitrary"), VMEM_LIMIT),
        name="conv_ffn_seq",
    )(x, shift, scale, gate, wu, wg, cw, cb.reshape(1, dff), wd, g.reshape(1, d), b.reshape(1, d))


def _ffn_rows_call(x, shift, scale, gate, h0, h1, wu, wg, cw, cb, wd, g, b, alpha):
    n, d = x.shape
    dff = wu.shape[1]
    full = lambda shape: pl.BlockSpec(shape, lambda i: (0, 0))
    return pl.pallas_call(
        functools.partial(_ffn_rows_kernel, alpha=alpha),
        grid=(1,),
        in_specs=[full((n, d)), full((n, d)), full((n, d)), full((n, d)), full((n, dff)), full((n, dff)),
                  full((d, dff)), full((d, dff)), full((3, dff)), full((1, dff)), full((dff, d)),
                  full((1, d)), full((1, d))],
        out_specs=[full((n, d)), full((n, dff))],
        out_shape=[jax.ShapeDtypeStruct((n, d), F32), jax.ShapeDtypeStruct((n, dff), F32)],
        compiler_params=_cparams(("arbitrary",), VMEM_LIMIT),
        name="conv_ffn_rows",
    )(x, shift, scale, gate, h0, h1, wu, wg, cw, cb.reshape(1, dff), wd, g.reshape(1, d), b.reshape(1, d))


def _idxsel_kernel(pt_ref, qi_ref, wi_ref, tail_ref, expand_ref, *rest, pages, kk):
    page_refs = rest[:pages]
    bias_ref, bias_self_ref, sc_ref = rest[pages:]
    j = pl.program_id(1)
    nj = pl.num_programs(1)
    n_pages, page = sc_ref.shape
    qi = qi_ref[...]
    wi = wi_ref[:, 0:1]
    rows = []
    for k in range(pages):
        logits = _dot3(qi, page_refs[k][...], _NT) * (D_IDX ** -0.5)
        rows.append(jnp.sum(jnp.maximum(logits, 0.0) * wi, axis=0, keepdims=True))
    sc_ref[pl.ds(pl.multiple_of(j * pages, pages), pages), :] = jnp.concatenate(rows, axis=0)

    @pl.when(j == nj - 1)
    def _():
        ki = tail_ref[:, 0:D_IDX]
        logit_self = jnp.sum(qi * ki, axis=-1, keepdims=True) * (D_IDX ** -0.5)
        s_self = jnp.sum(jnp.maximum(logit_self, 0.0) * wi, axis=0, keepdims=True)
        key_self = _sort_key(s_self)
        key = _sort_key(sc_ref[...])

        def total(x):
            return jnp.sum(jnp.sum(x, axis=0, keepdims=True), axis=1, keepdims=True)

        def count(pred):
            return total(pred(key).astype(jnp.int32)) + pred(key_self).astype(jnp.int32)

        thr = _kth_largest_key(lambda cand: count(lambda k: k >= cand), kk, (1, 1))
        need = (kk - count(lambda k: k > thr)).astype(F32)
        eq = key == thr
        eqb = eq.astype(BF16)
        in_page = (lax.broadcasted_iota(jnp.int32, (page, page), 0)
                   <= lax.broadcasted_iota(jnp.int32, (page, page), 1)).astype(BF16)
        earlier = (lax.broadcasted_iota(jnp.int32, (n_pages, n_pages), 1)
                   < lax.broadcasted_iota(jnp.int32, (n_pages, n_pages), 0)).astype(F32)
        per_page = jnp.broadcast_to(jnp.sum(eq.astype(F32), axis=1, keepdims=True), (n_pages, page))
        before = _dot(earlier.astype(BF16), per_page.astype(BF16))[:, 0:1]
        cum = _dot(eqb, in_page) + before
        sel = (key > thr) | (eq & (cum <= need))
        expanded = _dot(sel.astype(BF16), expand_ref[...])
        bias_ref[...] = jnp.where(expanded > 0.5, 0.0, NEG_INF).astype(F32)
        n_eq = total(eq.astype(F32))
        sel_self = (key_self > thr) | ((key_self == thr) & (n_eq + 1.0 <= need))
        bias_self_ref[...] = jnp.broadcast_to(jnp.where(sel_self, 0.0, NEG_INF).astype(F32), (1, LANE))


def _idxsel_call(page_table_flat, qi8, wi8, tail3, expand, cache_idx, layer, n_pages, pages, kk):
    bs = qi8.shape[0]
    page = cache_idx.shape[2]

    def page_spec(k):
        return pl.BlockSpec((None, None, page, D_IDX),
                            lambda s, j, pt: (layer, pt[s * n_pages + j * pages + k], 0, 0))

    per_seq = lambda shape: pl.BlockSpec((None,) + shape, lambda s, j, pt: (s,) + (0,) * len(shape))
    return pl.pallas_call(
        functools.partial(_idxsel_kernel, pages=pages, kk=kk),
        grid_spec=pltpu.PrefetchScalarGridSpec(
            num_scalar_prefetch=1,
            grid=(bs, n_pages // pages),
            in_specs=[per_seq((8, D_IDX)), per_seq((8, LANE)), per_seq((1, TAILW)),
                      pl.BlockSpec((page, page * N_HEADS), lambda s, j, pt: (0, 0))]
                     + [page_spec(k) for k in range(pages)],
            out_specs=[per_seq((n_pages, page * N_HEADS)), per_seq((1, LANE))],
            scratch_shapes=[pltpu.VMEM((n_pages, page), F32)]),
        out_shape=[jax.ShapeDtypeStruct((bs, n_pages, page * N_HEADS), F32),
                   jax.ShapeDtypeStruct((bs, 1, LANE), F32)],
        compiler_params=_cparams(("arbitrary", "arbitrary"), VMEM_LIMIT),
        name="decode_index_select",
    )(page_table_flat, qi8, wi8, tail3, expand, *([cache_idx] * pages))


def _dec_kernel(pt_ref, q_ref, kvn_ref, hmask_ref, bias_ref, bself_ref, *rest, pages, pages_per_block):
    page_refs = rest[:pages]
    fin_ref, pm_ref, pl_ref, pa_ref, xb_ref, s_ref, p_ref, m_ref, l_ref, acc_ref = rest[pages:]
    j = pl.program_id(1)
    nj = pl.num_programs(1)
    row = lax.broadcasted_iota(jnp.int32, (DEC_ROWS, 1), 0)
    is_dsa = row >= 24
    is_moba = (row >= 16) & (row < 24)
    scale = jnp.where(row < 16, DQK_A ** -0.5, HEAD_DIM ** -0.5).astype(F32)
    qf = q_ref[...] * scale
    qb = qf.astype(BF16)

    @pl.when(j == 0)
    def _():
        kvn = kvn_ref[...]
        s_self = jnp.sum(qf * kvn, axis=-1, keepdims=True) + jnp.where(is_dsa, bself_ref[:, 0:1], 0.0)
        m0 = jnp.maximum(s_self, M_INIT)
        p0 = jnp.exp(s_self - m0)
        m_ref[...] = jnp.broadcast_to(m0, m_ref.shape)
        l_ref[...] = jnp.broadcast_to(p0, l_ref.shape)
        acc_ref[...] = p0 * kvn

    blocks = pages // pages_per_block
    for k in range(pages):
        xb = page_refs[k][...].astype(BF16)
        xb_ref[k] = xb
        s_ref[k] = _dot(qb, xb, _NT) + hmask_ref[...] + jnp.where(is_dsa, bias_ref[k:k + 1, :], 0.0)
    block_max = []
    for kb in range(blocks):
        mx = jnp.max(s_ref[kb * pages_per_block], axis=-1, keepdims=True)
        for kp in range(1, pages_per_block):
            mx = jnp.maximum(mx, jnp.max(s_ref[kb * pages_per_block + kp], axis=-1, keepdims=True))
        block_max.append(mx)
    m_old = m_ref[:, 0:1]
    m_all = m_old
    for mx in block_max:
        m_all = jnp.maximum(m_all, mx)
    alpha = jnp.exp(m_old - m_all)
    block_m, block_l = [], []
    for kb in range(blocks):
        m_use = jnp.where(is_moba, jnp.maximum(block_max[kb], M_INIT), m_all)
        lb = jnp.zeros((DEC_ROWS, 1), F32)
        for kp in range(pages_per_block):
            k = kb * pages_per_block + kp
            p = jnp.exp(s_ref[k] - m_use)
            lb = lb + jnp.sum(p, axis=-1, keepdims=True)
            p_ref[k] = p.astype(BF16)
        block_m.append(m_use)
        block_l.append(lb)
    l_new = alpha * l_ref[:, 0:1]
    acc_new = alpha * acc_ref[...]
    for kb in range(blocks):
        accb = _dot(p_ref[kb * pages_per_block], xb_ref[kb * pages_per_block])
        for kp in range(1, pages_per_block):
            k = kb * pages_per_block + kp
            accb = accb + _dot(p_ref[k], xb_ref[k])
        l_new = l_new + block_l[kb]
        acc_new = acc_new + accb
        blk = j * blocks + kb
        pm_ref[pl.ds(blk, 1)] = jnp.broadcast_to(block_m[kb][16:24], (8, LANE)).reshape(1, 8, LANE)
        pl_ref[pl.ds(blk, 1)] = jnp.broadcast_to(block_l[kb][16:24], (8, LANE)).reshape(1, 8, LANE)
        pa_ref[pl.ds(blk, 1)] = accb[16:24].reshape(1, 8, ROW)
    m_ref[...] = jnp.broadcast_to(m_all, m_ref.shape)
    l_ref[...] = jnp.broadcast_to(l_new, l_ref.shape)
    acc_ref[...] = acc_new

    @pl.when(j == nj - 1)
    def _():
        fin_ref[...] = acc_ref[...] / l_ref[:, 0:1]


def _dec_call(page_table_flat, q_dec, kvn32, hmask, bias, bias_self, cache_rows, layer, n_pages, pages,
              pages_per_block):
    bs = q_dec.shape[0]
    rows_per_page = cache_rows.shape[2]
    nblk = n_pages // pages_per_block

    def page_spec(k):
        return pl.BlockSpec((None, None, rows_per_page, ROW),
                            lambda s, j, pt: (layer, pt[s * n_pages + j * pages + k], 0, 0))

    per_seq = lambda shape: pl.BlockSpec((None,) + shape, lambda s, j, pt: (s,) + (0,) * len(shape))
    return pl.pallas_call(
        functools.partial(_dec_kernel, pages=pages, pages_per_block=pages_per_block),
        grid_spec=pltpu.PrefetchScalarGridSpec(
            num_scalar_prefetch=1,
            grid=(bs, n_pages // pages),
            in_specs=[per_seq((DEC_ROWS, ROW)), per_seq((DEC_ROWS, ROW)),
                      pl.BlockSpec((DEC_ROWS, rows_per_page), lambda s, j, pt: (0, 0)),
                      pl.BlockSpec((None, pages, rows_per_page), lambda s, j, pt: (s, j, 0)),
                      per_seq((1, LANE))] + [page_spec(k) for k in range(pages)],
            out_specs=[per_seq((DEC_ROWS, ROW)), per_seq((nblk, 8, LANE)), per_seq((nblk, 8, LANE)),
                       per_seq((nblk, 8, ROW))],
            scratch_shapes=[pltpu.VMEM((pages, rows_per_page, ROW), BF16),
                            pltpu.VMEM((pages, DEC_ROWS, rows_per_page), F32),
                            pltpu.VMEM((pages, DEC_ROWS, rows_per_page), BF16),
                            pltpu.VMEM((DEC_ROWS, LANE), F32), pltpu.VMEM((DEC_ROWS, LANE), F32),
                            pltpu.VMEM((DEC_ROWS, ROW), F32)]),
        out_shape=[jax.ShapeDtypeStruct((bs, DEC_ROWS, ROW), F32),
                   jax.ShapeDtypeStruct((bs, nblk, 8, LANE), F32),
                   jax.ShapeDtypeStruct((bs, nblk, 8, LANE), F32),
                   jax.ShapeDtypeStruct((bs, nblk, 8, ROW), F32)],
        compiler_params=_cparams(("arbitrary", "arbitrary"), VMEM_LIMIT),
        name="decode_paged_attention",
    )(page_table_flat, q_dec, kvn32, hmask, bias, bias_self, *([cache_rows] * pages))


def _fin_kernel(lam_ref, fin_ref, pm_ref, pl_ref, pa_ref, q_ref, kvn_ref, g_ref,
                oa_ref, ob_ref, oc_ref, *, nblk, norm_scale):
    lam = lam_ref[0]
    for h in range(H_A):
        o = fin_ref[2 * h:2 * h + 1, HEAD_DIM:ROW] - lam * fin_ref[2 * h + 1:2 * h + 2, HEAD_DIM:ROW]
        o = o * lax.rsqrt(jnp.mean(o * o, axis=-1, keepdims=True) + LN_EPS)
        oa_ref[:, h * HEAD_DIM:(h + 1) * HEAD_DIM] = (o * g_ref[...] * norm_scale).astype(oa_ref.dtype)
    for h in range(H_C):
        oc_ref[:, h * HEAD_DIM:(h + 1) * HEAD_DIM] = fin_ref[24 + h:25 + h, HEAD_DIM:ROW].astype(oc_ref.dtype)

    blk = lax.broadcasted_iota(jnp.int32, (nblk, 1), 0)
    ksel = min(MOBA_TOPK, nblk)
    for h in range(H_B):
        qh = q_ref[:, Q_QB + h * HEAD_DIM:Q_QB + (h + 1) * HEAD_DIM]
        mean = pa_ref[:, H_B + h, 0:HEAD_DIM] * (1.0 / MOBA_BLOCK)
        gate = jnp.sum(mean * qh, axis=-1, keepdims=True)
        sel = jnp.zeros((nblk, 1), jnp.bool_)
        for _ in range(ksel):
            best = jnp.max(gate, axis=0, keepdims=True)
            first = jnp.min(jnp.where(gate == best, blk, nblk), axis=0, keepdims=True)
            pick = blk == first
            sel = sel | pick
            gate = jnp.where(pick, -jnp.inf, gate)
        k_self = kvn_ref[H_A + h:H_A + h + 1, 0:HEAD_DIM]
        v_self = kvn_ref[H_A + h:H_A + h + 1, HEAD_DIM:ROW]
        s_self = jnp.sum(qh * k_self, axis=-1, keepdims=True) * (HEAD_DIM ** -0.5)
        m_b = pm_ref[:, h, 0:1]
        l_b = pl_ref[:, h, 0:1]
        m_star = jnp.maximum(jnp.max(jnp.where(sel, m_b, M_INIT), axis=0, keepdims=True), s_self)
        w_b = jnp.where(sel, jnp.exp(m_b - m_star), 0.0)
        w_self = jnp.exp(s_self - m_star)
        l_star = jnp.sum(w_b * l_b, axis=0, keepdims=True) + w_self
        acc = jnp.sum(w_b * pa_ref[:, h, HEAD_DIM:ROW], axis=0, keepdims=True) + w_self * v_self
        ob_ref[:, h * HEAD_DIM:(h + 1) * HEAD_DIM] = (acc / l_star).astype(ob_ref.dtype)


def _fin_call(lam, fin, pm, pl_, pa, q_s, kvn16, g, norm_scale):
    bs, nblk = pm.shape[0], pm.shape[1]
    per_seq = lambda a: pl.BlockSpec((None,) + a.shape[1:], lambda s: (s,) + (0,) * (a.ndim - 1))
    q3 = q_s.reshape(bs, 1, QW)
    out = lambda w: pl.BlockSpec((None, 1, w), lambda s: (s, 0, 0))
    return pl.pallas_call(
        functools.partial(_fin_kernel, nblk=nblk, norm_scale=norm_scale),
        grid=(bs,),
        in_specs=[pl.BlockSpec(memory_space=pltpu.SMEM), per_seq(fin), per_seq(pm), per_seq(pl_), per_seq(pa),
                  per_seq(q3), per_seq(kvn16), pl.BlockSpec((1, HEAD_DIM), lambda s: (0, 0))],
        out_specs=[out(H_A * HEAD_DIM), out(H_B * HEAD_DIM), out(H_C * HEAD_DIM)],
        out_shape=[jax.ShapeDtypeStruct((bs, 1, H_A * HEAD_DIM), BF16),
                   jax.ShapeDtypeStruct((bs, 1, H_B * HEAD_DIM), BF16),
                   jax.ShapeDtypeStruct((bs, 1, H_C * HEAD_DIM), BF16)],
        compiler_params=_cparams(("arbitrary",)),
        name="decode_finish",
    )(lam, fin, pm, pl_, pa, q3, kvn16, g)


def kernel(x_prompt, x_sample, cache_kv, cache_index_k, page_table, state_ffn_conv, c_prompt, c_sample, w_in, w_out, diff_lambda, diff_norm_g, w_ada_attn, b_ada_attn, ln1_g, ln1_b, w_ada_ffn, b_ada_ffn, w_up, w_gate, conv_w, conv_b, w_down, ln2_g, ln2_b):
    b, t, d = x_prompt.shape
    bs, ts, _ = x_sample.shape
    depth, n_pool, page, n_heads, row_w = cache_kv.shape
    n_pages = page_table.shape[1]
    past = n_pages * page
    assert ts == 1 and n_heads == N_HEADS and row_w == ROW and d == N_HEADS * HEAD_DIM
    assert conv_w.shape[1] == 3 and MOBA_BLOCK % page == 0 and past % MOBA_BLOCK == 0
    assert t % MOBA_BLOCK == 0 and bs % SUBLANE == 0 and page == LANE
    alpha = (2 * depth) ** 0.25
    pages_per_block = MOBA_BLOCK // page
    tq = MOBA_BLOCK
    tm_proj = min(512, t)
    tm_ffn = min(256, t)
    topk_prompt = min(DSA_TOPK, t // 4)
    n_keys_s = past + ts
    kk_sample = min(min(DSA_TOPK, n_keys_s // 4), n_keys_s)
    dec_pages = math.gcd(n_pages, 8)
    assert dec_pages % pages_per_block == 0 and dec_pages % SUBLANE == 0

    c_all = jnp.concatenate([c_prompt, c_sample], axis=0)
    m_attn = _ada_call(c_all, w_ada_attn, b_ada_attn)
    m_ffn = _ada_call(c_all, w_ada_ffn, b_ada_ffn)

    def mods(m, l):
        parts = []
        for k in range(3):
            mk = m[l, :, k * d:(k + 1) * d]
            parts.append((mk[:b].reshape(b, 1, d), mk[b:].reshape(1, bs, d)))
        return parts

    tabs_p = _rope_tables(jnp.arange(t, dtype=jnp.int32))
    tabs_s = _rope_tables(jnp.full((bs,), past, jnp.int32))
    pt_flat = page_table.reshape(-1).astype(jnp.int32)
    cache_rows = cache_kv.reshape(depth, n_pool, page * N_HEADS, ROW)
    dec_qidx = jnp.asarray(_DEC_QIDX)
    head_of_row = jnp.asarray(DEC_HEAD_OF_ROW)
    lane_head = np.arange(page * N_HEADS) % N_HEADS
    hmask = jnp.asarray(np.where(lane_head[None, :] == DEC_HEAD_OF_ROW[:, None], 0.0, NEG_INF).astype(np.float32))
    expand = jnp.asarray((np.arange(page * N_HEADS)[None, :] // N_HEADS == np.arange(page)[:, None])
                         .astype(np.float32)).astype(BF16)

    xp = x_prompt
    xs = x_sample.reshape(1, bs, d)
    kvp, kvs, ikp, iks, cvp, cvs = [], [], [], [], [], []
    for l in range(depth):
        dl = diff_lambda[l].astype(F32)
        lam_init = 0.8 - 0.6 * math.exp(-0.3 * l)
        lam = (jnp.exp(jnp.sum(dl[0] * dl[1])) - jnp.exp(jnp.sum(dl[2] * dl[3])) + lam_init).reshape(1)
        norm_scale = 1.0 - lam_init
        wl = w_in[l]
        wp = jnp.concatenate([wl[:, a:a + n] for a, n in _PERM_SEGMENTS]
                             + [jnp.zeros((d, WP - D_IN), F32)], axis=1).astype(BF16)
        wo = w_out[l].astype(BF16)
        wu, wg, wd = w_up[l].astype(BF16), w_gate[l].astype(BF16), w_down[l].astype(BF16)
        gnorm = diff_norm_g[l].reshape(1, HEAD_DIM)
        (shift_p, shift_s), (scale_p, scale_s), (gate_p, gate_s) = mods(m_attn, l)
        (fshift_p, fshift_s), (fscale_p, fscale_s), (fgate_p, fgate_s) = mods(m_ffn, l)

        q, kv, kva, kvb, kvc, tail = _proj_call(xp, shift_p, scale_p, wp, tabs_p, tm_proj)
        oa = _diff_call(lam, q, kva, gnorm.reshape(HEAD_DIM, 1), tq, norm_scale)
        ob = _moba_call(q, kvb)
        oc = _dsa_call(q, tail, kvc, tq, topk_prompt)
        x1 = _out_call(oa, ob, oc, xp, gate_p, wo, ln1_g[l], ln1_b[l], tm_proj, alpha)
        xp, conv_p = _ffn_seq_call(x1, fshift_p, fscale_p, fgate_p, wu, wg, conv_w[l], conv_b[l], wd,
                                   ln2_g[l], ln2_b[l], tm_ffn, alpha)
        kvp.append(kv.reshape(b, t, N_HEADS, ROW))
        ikp.append(tail[:, :, :D_IDX])
        cvp.append(conv_p)

        q_s, kv_s, _, _, _, tail_s = _proj_call(xs, shift_s, scale_s, wp, tabs_s, bs)
        q_s, kv_s, tail_s = q_s[0], kv_s[0], tail_s[0]
        qi8 = jnp.pad(q_s[:, Q_QI:Q_QI + H_IDX * D_IDX].reshape(bs, H_IDX, D_IDX), ((0, 0), (0, 4), (0, 0)))
        wi8 = jnp.broadcast_to(jnp.pad(tail_s[:, D_IDX:D_IDX + H_IDX], ((0, 0), (0, 4)))[:, :, None],
                               (bs, 8, LANE))
        bias, bias_self = _idxsel_call(pt_flat, qi8, wi8, tail_s.reshape(bs, 1, TAILW), expand, cache_index_k,
                                       l, n_pages, math.gcd(n_pages, 16), kk_sample)
        q_dec = jnp.concatenate([q_s, jnp.zeros((bs, 1), F32)], axis=1)[:, dec_qidx].reshape(bs, DEC_ROWS, ROW)
        kvn16 = kv_s.reshape(bs, N_HEADS, ROW)
        kvn32 = kvn16[:, head_of_row]
        fin, pm, pl_, pa = _dec_call(pt_flat, q_dec, kvn32, hmask, bias, bias_self, cache_rows, l, n_pages,
                                     dec_pages, pages_per_block)
        oa_s, ob_s, oc_s = _fin_call(lam, fin, pm, pl_, pa, q_s, kvn16, gnorm, norm_scale)
        x1s = _out_call(oa_s.reshape(1, bs, -1), ob_s.reshape(1, bs, -1), oc_s.reshape(1, bs, -1), xs, gate_s,
                        wo, ln1_g[l], ln1_b[l], bs, alpha)
        hist = state_ffn_conv[l]
        x2s, g_s = _ffn_rows_call(x1s[0], fshift_s[0], fscale_s[0], fgate_s[0], hist[:, 0], hist[:, 1],
                                  wu, wg, conv_w[l], conv_b[l], wd, ln2_g[l], ln2_b[l], alpha)
        xs = x2s.reshape(1, bs, d)
        kvs.append(kv_s.reshape(bs, ts, N_HEADS, ROW))
        iks.append(tail_s[:, :D_IDX].reshape(bs, ts, D_IDX))
        cvs.append(jnp.stack([hist[:, 1], g_s], axis=1))

    return (xp, xs.reshape(bs, ts, d), jnp.stack(kvp), jnp.stack(kvs), jnp.stack(ikp), jnp.stack(iks),
            jnp.stack(cvp), jnp.stack(cvs))
```
